```python
import jax
import jax.numpy as jnp
from jax import lax
import numpy as np

D_MODEL = 2048
BATCH = 8
SEQ = 2048
DEPTH = 2

GRID_W = 64
CTX_LEN = 256
NORM_EPS = 1e-6
NEG_INF = -1e30
Q_BLOCK = 128

MLA_HEADS = D_MODEL // 256
MLA_Q_LORA = D_MODEL // 4
MLA_KV_LORA = D_MODEL // 8
MLA_NOPE = 128
MLA_ROPE = 64
MLA_V = 128
MLA_SCALE = (MLA_NOPE + MLA_ROPE) ** -0.5
ROPE_THETA = 10000.0
ROPE_AX_PAIRS = MLA_ROPE // 4

LRU_WIDTH = D_MODEL // 2
LRU_BLOCKS = 8
LRU_BLOCK_W = LRU_WIDTH // LRU_BLOCKS
LRU_CONV = 4
LRU_PAD = (2, 1)
LRU_C = 8.0

MIX_WIDTH = MLA_HEADS * MLA_V + LRU_WIDTH
IN_SPLITS = (MLA_Q_LORA, MLA_Q_LORA + MLA_KV_LORA + MLA_ROPE, MLA_Q_LORA + MLA_KV_LORA + MLA_ROPE + LRU_WIDTH)
IN_COLS = MLA_Q_LORA + MLA_KV_LORA + MLA_ROPE + 2 * LRU_WIDTH

NA_HEADS = 16
NA_HEAD_DIM = D_MODEL // NA_HEADS
NA_WIDTH = NA_HEADS * NA_HEAD_DIM
NA_SCALE = NA_HEAD_DIM ** -0.5
NA_ROWS = 8
NA_COLS = 16

FFN_HIDDEN = (D_MODEL * 11) // 4
FFN_CONV = 3
FFN_PAD = (1, 1)

kernel_name = 'hybrid_mla_rglru_natten_dit'


def rms_norm(x, g):
    xf = x.astype(jnp.float32)
    y = xf * lax.rsqrt(jnp.mean(xf * xf, axis=-1, keepdims=True) + NORM_EPS)
    return (y * g.astype(jnp.float32)).astype(x.dtype)


def modulate(h, shift, scale):
    return h * (1.0 + scale) + shift


def rope_tables(row, col, dtype):
    inv = ROPE_THETA ** (-jnp.arange(ROPE_AX_PAIRS, dtype=jnp.float32) / ROPE_AX_PAIRS)
    ar = row.astype(jnp.float32)[:, None] * inv
    ac = col.astype(jnp.float32)[:, None] * inv
    ang = jnp.concatenate([ar, ar, ac, ac], axis=-1)
    return jnp.cos(ang).astype(dtype), jnp.sin(ang).astype(dtype)


def apply_rope(x, cos, sin):
    x4 = x.reshape(x.shape[:-1] + (2, 2, ROPE_AX_PAIRS))
    rot = jnp.stack([-x4[..., 1, :], x4[..., 0, :]], axis=-2).reshape(x.shape)
    return x * cos[:, None, :] + rot * sin[:, None, :]


def softmax_attention(q, k, v, scale):
    s = jnp.einsum('bqhd,bkhd->bhqk', q, k).astype(jnp.float32) * scale
    p = jax.nn.softmax(s, axis=-1).astype(v.dtype)
    return jnp.einsum('bhqk,bkhd->bqhd', p, v)


def blockwise_attention(q, k, v, scale):
    B, T, H, dk = q.shape
    qb = jnp.moveaxis(q.reshape(B, T // Q_BLOCK, Q_BLOCK, H, dk), 1, 0)
    o = lax.map(lambda qi: softmax_attention(qi, k, v, scale), qb)
    return jnp.moveaxis(o, 0, 1).reshape(B, T, H, v.shape[-1])


def dwconv(x, w, b, pad):
    T = x.shape[1]
    xp = jnp.pad(x, ((0, 0), pad, (0, 0)))
    out = b
    for tap in range(w.shape[0]):
        out = out + xp[:, tap:tap + T] * w[tap]
    return out


def mla_queries(cq, q_norm, w_uq, cos, sin):
    B, T, _ = cq.shape
    q = (rms_norm(cq, q_norm) @ w_uq).reshape(B, T, MLA_HEADS, MLA_NOPE + MLA_ROPE)
    if cos is not None:
        q = jnp.concatenate([q[..., :MLA_NOPE], apply_rope(q[..., MLA_NOPE:], cos, sin)], axis=-1)
    return q


def mla_keys_values(ckv, kv_norm, w_ukv, cos, sin):
    B, T, _ = ckv.shape
    c_lat = rms_norm(ckv[..., :MLA_KV_LORA], kv_norm)
    k_r = ckv[..., MLA_KV_LORA:][:, :, None, :]
    kv = (c_lat @ w_ukv).reshape(B, T, MLA_HEADS, MLA_NOPE + MLA_V)
    if cos is not None:
        k_r = apply_rope(k_r, cos, sin)
    k = jnp.concatenate([kv[..., :MLA_NOPE], jnp.broadcast_to(k_r, (B, T, MLA_HEADS, MLA_ROPE))], axis=-1)
    return k, kv[..., MLA_NOPE:]


def rglru_coeffs(xc, ga_w, ga_b, gx_w, gx_b, lam):
    B, T, W = xc.shape
    xf = xc.astype(jnp.float32)
    xb = xf.reshape(B, T, LRU_BLOCKS, LRU_BLOCK_W)
    r = jax.nn.sigmoid(jnp.einsum('btnd,nde->btne', xb, ga_w.astype(jnp.float32)).reshape(B, T, W) + ga_b.astype(jnp.float32))
    i = jax.nn.sigmoid(jnp.einsum('btnd,nde->btne', xb, gx_w.astype(jnp.float32)).reshape(B, T, W) + gx_b.astype(jnp.float32))
    log_a = -LRU_C * r * jax.nn.softplus(-lam.astype(jnp.float32))
    a = jnp.exp(log_a)
    b = jnp.sqrt(-jnp.expm1(2.0 * log_a)) * (i * xf)
    return a, b


def linear_scan(a, b, h0):
    def combine(left, right):
        a1, b1 = left
        a2, b2 = right
        return a1 * a2, a2 * b1 + b2
    a_cum, b_cum = lax.associative_scan(combine, (a, b), axis=1)
    return a_cum * h0[:, None, :] + b_cum


def rglru_bidir(xl, xc, ga_w, ga_b, gx_w, gx_b, lam, with_ctx):
    B, _, W = xl.shape
    zeros = jnp.zeros((B, W), jnp.float32)
    a, b = rglru_coeffs(xc, ga_w[0], ga_b[0], gx_w[0], gx_b[0], lam[0])
    hc_f = linear_scan(a, b, zeros)
    a, b = rglru_coeffs(xl, ga_w[0], ga_b[0], gx_w[0], gx_b[0], lam[0])
    hl_f = linear_scan(a, b, hc_f[:, -1])
    a, b = rglru_coeffs(jnp.flip(xc, 1), ga_w[1], ga_b[1], gx_w[1], gx_b[1], lam[1])
    hc_b = linear_scan(a, b, zeros)
    a, b = rglru_coeffs(jnp.flip(xl, 1), ga_w[1], ga_b[1], gx_w[1], gx_b[1], lam[1])
    hl_b = linear_scan(a, b, hc_b[:, -1])
    h_lat = (hl_f + jnp.flip(hl_b, 1)).astype(xl.dtype)
    h_ctx = (hc_f + jnp.flip(hc_b, 1)).astype(xc.dtype) if with_ctx else None
    return h_lat, h_ctx


def mla_rglru_mixer(h, hc, w_in, q_norm, w_uq, kv_norm, w_ukv, conv_w, conv_b, ga_w, ga_b, gx_w, gx_b, lam, w_out, cos, sin, with_ctx):
    B, T, _ = h.shape
    Tc = hc.shape[1]
    q_l, kv_l, rx_l, ry_l = jnp.split(h @ w_in, IN_SPLITS, axis=-1)
    q_c, kv_c, rx_c, ry_c = jnp.split(hc @ w_in, IN_SPLITS, axis=-1)
    q = mla_queries(q_l, q_norm, w_uq, cos, sin)
    k_l, v_l = mla_keys_values(kv_l, kv_norm, w_ukv, cos, sin)
    k_c, v_c = mla_keys_values(kv_c, kv_norm, w_ukv, None, None)
    att_l = blockwise_attention(q, jnp.concatenate([k_c, k_l], 1), jnp.concatenate([v_c, v_l], 1), MLA_SCALE)
    x_l = dwconv(rx_l, conv_w, conv_b, LRU_PAD)
    x_c = dwconv(rx_c, conv_w, conv_b, LRU_PAD)
    rec_l, rec_c = rglru_bidir(x_l, x_c, ga_w, ga_b, gx_w, gx_b, lam, with_ctx)
    out_l = jnp.concatenate([att_l.reshape(B, T, -1), rec_l * jax.nn.gelu(ry_l)], axis=-1) @ w_out
    if not with_ctx:
        return out_l, None
    att_c = softmax_attention(mla_queries(q_c, q_norm, w_uq, None, None), k_c, v_c, MLA_SCALE)
    out_c = jnp.concatenate([att_c.reshape(B, Tc, -1), rec_c * jax.nn.gelu(ry_c)], axis=-1) @ w_out
    return out_l, out_c


def na_mixer(h, hc, w_qkv, rel_bias, w_out, with_ctx):
    B, T, _ = h.shape
    Tc = hc.shape[1]
    rows = T // GRID_W
    kh = min(NA_ROWS, rows)
    n_band = kh * GRID_W
    qkv = (h @ w_qkv).reshape(B, T, 3, NA_HEADS, NA_HEAD_DIM)
    q, k, v = qkv[:, :, 0], qkv[:, :, 1], qkv[:, :, 2]
    kv_c = (hc @ w_qkv[:, NA_WIDTH:]).reshape(B, Tc, 2, NA_HEADS, NA_HEAD_DIM)
    k_c, v_c = kv_c[:, :, 0], kv_c[:, :, 1]
    kg = k.reshape(B, rows, GRID_W, NA_HEADS, NA_HEAD_DIM)
    vg = v.reshape(B, rows, GRID_W, NA_HEADS, NA_HEAD_DIM)
    qg = jnp.moveaxis(q.reshape(B, rows, GRID_W, NA_HEADS, NA_HEAD_DIM), 1, 0)
    col = jnp.arange(GRID_W)
    cs = jnp.clip(col - NA_COLS // 2, 0, GRID_W - NA_COLS)
    col_ok = (col[None, :] >= cs[:, None]) & (col[None, :] < cs[:, None] + NA_COLS)
    band_mask = jnp.broadcast_to(col_ok[:, None, :], (GRID_W, kh, GRID_W)).reshape(GRID_W, n_band)
    dc_idx = jnp.clip(col[None, :] - col[:, None] + NA_COLS - 1, 0, 2 * NA_COLS - 2)
    bias_tab = rel_bias.astype(jnp.float32)

    def row_block(args):
        r, q_row = args
        rs = jnp.clip(r - kh // 2, 0, rows - kh)
        k_band = lax.dynamic_slice_in_dim(kg, rs, kh, axis=1).reshape(B, n_band, NA_HEADS, NA_HEAD_DIM)
        v_band = lax.dynamic_slice_in_dim(vg, rs, kh, axis=1).reshape(B, n_band, NA_HEADS, NA_HEAD_DIM)
        dr_idx = rs + jnp.arange(kh) - r + NA_ROWS - 1
        bias = bias_tab[:, dr_idx[:, None, None], dc_idx[None, :, :]]
        bias = jnp.transpose(bias, (0, 2, 1, 3)).reshape(NA_HEADS, GRID_W, n_band)
        s_band = jnp.einsum('bqhd,bkhd->bhqk', q_row, k_band).astype(jnp.float32) * NA_SCALE + bias
        s_band = jnp.where(band_mask, s_band, NEG_INF)
        s_ctx = jnp.einsum('bqhd,bkhd->bhqk', q_row, k_c).astype(jnp.float32) * NA_SCALE
        p = jax.nn.softmax(jnp.concatenate([s_band, s_ctx], axis=-1), axis=-1).astype(v.dtype)
        return (jnp.einsum('bhqk,bkhd->bqhd', p[..., :n_band], v_band)
                + jnp.einsum('bhqk,bkhd->bqhd', p[..., n_band:], v_c))

    o = lax.map(row_block, (jnp.arange(rows), qg))
    o = jnp.moveaxis(o, 0, 1).reshape(B, T, NA_WIDTH)
    out_l = o @ w_out
    if not with_ctx:
        return out_l, None
    q_c = (hc @ w_qkv[:, :NA_WIDTH]).reshape(B, Tc, NA_HEADS, NA_HEAD_DIM)
    o_c = softmax_attention(q_c, k_c, v_c, NA_SCALE).reshape(B, Tc, NA_WIDTH)
    return out_l, o_c @ w_out


def conv_ffn(h, w_up, conv_w, conv_b, w_down):
    g, u = jnp.split(h @ w_up, 2, axis=-1)
    g = dwconv(g, conv_w, conv_b, FFN_PAD)
    return (jax.nn.gelu(g) * u) @ w_down


def setup_inputs(seed: int = 0) -> dict:
    key = jax.random.key(seed)
    ks = iter(jax.random.split(key, 40))
    ne = (DEPTH + 1) // 2
    no = DEPTH // 2

    def nrm(shape, scale):
        return jax.random.normal(next(ks), shape, jnp.float32) * scale

    def gain(shape):
        return 1.0 + nrm(shape, 0.02)

    u = jax.random.uniform(next(ks), (ne, 2, LRU_WIDTH), jnp.float32, minval=0.9, maxval=0.999)
    a0 = u ** (1.0 / LRU_C)
    lam = jnp.log(a0) - jnp.log1p(-a0)
    return {
        'x': nrm((BATCH, SEQ, D_MODEL), 1.0),
        'c': nrm((BATCH, D_MODEL), 1.0),
        'ctx': nrm((BATCH, CTX_LEN, D_MODEL), 1.0),
        'c_ctx': nrm((D_MODEL,), 1.0),
        'mod_w': nrm((DEPTH, D_MODEL, 6 * D_MODEL), 0.5 * D_MODEL ** -0.5),
        'mod_b': nrm((DEPTH, 6 * D_MODEL), 0.02),
        'norm_mix': gain((DEPTH, D_MODEL)),
        'norm_ffn': gain((DEPTH, D_MODEL)),
        'mla_w_in': nrm((ne, D_MODEL, IN_COLS), D_MODEL ** -0.5),
        'mla_q_norm': gain((ne, MLA_Q_LORA)),
        'mla_w_uq': nrm((ne, MLA_Q_LORA, MLA_HEADS * (MLA_NOPE + MLA_ROPE)), MLA_Q_LORA ** -0.5),
        'mla_kv_norm': gain((ne, MLA_KV_LORA)),
        'mla_w_ukv': nrm((ne, MLA_KV_LORA, MLA_HEADS * (MLA_NOPE + MLA_V)), MLA_KV_LORA ** -0.5),
        'lru_conv_w': nrm((ne, LRU_CONV, LRU_WIDTH), LRU_CONV ** -0.5),
        'lru_conv_b': nrm((ne, LRU_WIDTH), 0.02),
        'lru_gate_a_w': nrm((ne, 2, LRU_BLOCKS, LRU_BLOCK_W, LRU_BLOCK_W), LRU_BLOCK_W ** -0.5),
        'lru_gate_a_b': nrm((ne, 2, LRU_WIDTH), 0.02),
        'lru_gate_x_w': nrm((ne, 2, LRU_BLOCKS, LRU_BLOCK_W, LRU_BLOCK_W), LRU_BLOCK_W ** -0.5),
        'lru_gate_x_b': nrm((ne, 2, LRU_WIDTH), 0.02),
        'lru_lambda': lam,
        'mix_w_out': nrm((ne, MIX_WIDTH, D_MODEL), MIX_WIDTH ** -0.5),
        'na_w_qkv': nrm((no, D_MODEL, 3 * NA_WIDTH), D_MODEL ** -0.5),
        'na_rel_bias': nrm((no, NA_HEADS, 2 * NA_ROWS - 1, 2 * NA_COLS - 1), 0.1),
        'na_w_out': nrm((no, NA_WIDTH, D_MODEL), NA_WIDTH ** -0.5),
        'ffn_w_up': nrm((DEPTH, D_MODEL, 2 * FFN_HIDDEN), D_MODEL ** -0.5),
        'ffn_conv_w': nrm((DEPTH, FFN_CONV, FFN_HIDDEN), FFN_CONV ** -0.5),
        'ffn_conv_b': nrm((DEPTH, FFN_HIDDEN), 0.02),
        'ffn_w_down': nrm((DEPTH, FFN_HIDDEN, D_MODEL), FFN_HIDDEN ** -0.5),
        'final_norm': gain((D_MODEL,)),
    }


def reference(x, c, ctx, c_ctx, mod_w, mod_b, norm_mix, norm_ffn,
              mla_w_in, mla_q_norm, mla_w_uq, mla_kv_norm, mla_w_ukv,
              lru_conv_w, lru_conv_b, lru_gate_a_w, lru_gate_a_b, lru_gate_x_w, lru_gate_x_b, lru_lambda,
              mix_w_out, na_w_qkv, na_rel_bias, na_w_out,
              ffn_w_up, ffn_conv_w, ffn_conv_b, ffn_w_down, final_norm):
    T = x.shape[1]
    pos = jnp.arange(T)
    cos, sin = rope_tables(pos // GRID_W, pos % GRID_W, x.dtype)
    s_lat = jax.nn.silu(c)
    s_ctx = jax.nn.silu(c_ctx)
    h_ctx = ctx
    for i in range(DEPTH):
        with_ctx = i < DEPTH - 1
        sh1, sc1, g1, sh2, sc2, g2 = jnp.split((s_lat @ mod_w[i] + mod_b[i])[:, None, :], 6, axis=-1)
        csh1, csc1, cg1, csh2, csc2, cg2 = jnp.split(s_ctx @ mod_w[i] + mod_b[i], 6, axis=-1)
        hx = modulate(rms_norm(x, norm_mix[i]), sh1, sc1)
        hc = modulate(rms_norm(h_ctx, norm_mix[i]), csh1, csc1)
        j = i // 2
        if i % 2 == 0:
            out_x, out_c = mla_rglru_mixer(hx, hc, mla_w_in[j], mla_q_norm[j], mla_w_uq[j], mla_kv_norm[j], mla_w_ukv[j],
                                           lru_conv_w[j], lru_conv_b[j], lru_gate_a_w[j], lru_gate_a_b[j],
                                           lru_gate_x_w[j], lru_gate_x_b[j], lru_lambda[j], mix_w_out[j],
                                           cos, sin, with_ctx)
        else:
            out_x, out_c = na_mixer(hx, hc, na_w_qkv[j], na_rel_bias[j], na_w_out[j], with_ctx)
        x = x + g1 * out_x
        x = x + g2 * conv_ffn(modulate(rms_norm(x, norm_ffn[i]), sh2, sc2),
                              ffn_w_up[i], ffn_conv_w[i], ffn_conv_b[i], ffn_w_down[i])
        if with_ctx:
            h_ctx = h_ctx + cg1 * out_c
            h_ctx = h_ctx + cg2 * conv_ffn(modulate(rms_norm(h_ctx, norm_ffn[i]), csh2, csc2),
                                           ffn_w_up[i], ffn_conv_w[i], ffn_conv_b[i], ffn_w_down[i])
    return rms_norm(x, final_norm)
```

```python
import functools

import jax
import jax.numpy as jnp
from jax import lax
from jax.experimental import pallas as pl
from jax.experimental.pallas import tpu as pltpu

D_MODEL = 2048
DEPTH = 2
GRID_W = 64
NORM_EPS = 1e-6
NEG_INF = -1e30

MLA_HEADS = D_MODEL // 256
MLA_Q_LORA = D_MODEL // 4
MLA_KV_LORA = D_MODEL // 8
MLA_NOPE = 128
MLA_ROPE = 64
MLA_V = 128
MLA_SCALE = (MLA_NOPE + MLA_ROPE) ** -0.5
ROPE_THETA = 10000.0
ROPE_AX_PAIRS = MLA_ROPE // 4

LRU_WIDTH = D_MODEL // 2
LRU_BLOCKS = 8
LRU_BLOCK_W = LRU_WIDTH // LRU_BLOCKS
LRU_CONV = 4
LRU_C = 8.0

NA_HEADS = 16
NA_HEAD_DIM = D_MODEL // NA_HEADS
NA_WIDTH = NA_HEADS * NA_HEAD_DIM
NA_SCALE = NA_HEAD_DIM ** -0.5
NA_ROWS = 8
NA_COLS = 16

FFN_HIDDEN = (D_MODEL * 11) // 4

LANES = 128
SUBLANES = 8
VMEM_BYTES = 64 * 1024 * 1024
ROPE_PAD = LANES
MLA_QK = MLA_NOPE + ROPE_PAD

BF16 = jnp.bfloat16
F32 = jnp.float32


def _compiler_params(semantics, vmem_bytes):
    limit = min(int(vmem_bytes), VMEM_BYTES - 4 * 1024 * 1024)
    return pltpu.CompilerParams(dimension_semantics=semantics, vmem_limit_bytes=limit)


def _resident(shape, index_map):
    return pl.BlockSpec(shape, index_map, pipeline_mode=pl.Buffered(1))


def _rms(x, g):
    return x * lax.rsqrt(jnp.mean(x * x, axis=-1, keepdims=True) + NORM_EPS) * g


def _norm_mod(x, gain, shift, scale):
    return _rms(x, gain) * (1.0 + scale) + shift


def _gelu(x):
    return 0.5 * x * (1.0 + jnp.tanh(0.7978845608028654 * (x + 0.044715 * (x * x * x))))


def _dot(a, b):
    return jnp.dot(a, b, preferred_element_type=F32)


def _dot_nt(a, b):
    return lax.dot_general(a, b, (((1,), (1,)), ((), ())), preferred_element_type=F32)


def _mod_kernel(c_ref, w_ref, b_ref, o_ref):
    c = c_ref[...]
    s = (c * jax.nn.sigmoid(c)).astype(BF16)
    o_ref[0] = _dot(s, w_ref[0].astype(BF16)) + b_ref[0]


def _modulation(cond, mod_w, mod_b, tn=1024):
    rows = cond.shape[0]
    depth, d, n = mod_w.shape
    return pl.pallas_call(
        _mod_kernel,
        grid=(depth, n // tn),
        in_specs=[
            pl.BlockSpec((rows, d), lambda l, j: (0, 0)),
            pl.BlockSpec((1, d, tn), lambda l, j: (l, 0, j)),
            pl.BlockSpec((1, 1, tn), lambda l, j: (l, 0, j)),
        ],
        out_specs=pl.BlockSpec((1, rows, tn), lambda l, j: (l, 0, j)),
        out_shape=jax.ShapeDtypeStruct((depth, rows, n), F32),
        compiler_params=_compiler_params(("parallel", "parallel"), 40 * 2**20),
        name="modulation",
    )(cond, mod_w, mod_b.reshape(depth, 1, n))


def _rope(x, cos, sin_a, sin_b):
    return x * cos + pltpu.roll(x, LANES - ROPE_AX_PAIRS, 1) * sin_a + pltpu.roll(x, ROPE_AX_PAIRS, 1) * sin_b


def _premix0_kernel(x_ref, gain_ref, sh_ref, sc_ref, win_ref, qn_ref, wuq_ref, kvn_ref, wukv_ref,
                    cos_ref, sina_ref, sinb_ref, q_ref, k_ref, v_ref, rx_ref, ry_ref):
    h = _norm_mod(x_ref[...], gain_ref[...], sh_ref[0], sc_ref[0]).astype(BF16)
    p = _dot(h, win_ref[...])
    o_kv = MLA_Q_LORA
    o_kr = o_kv + MLA_KV_LORA
    o_rx = o_kr + ROPE_PAD
    o_ry = o_rx + LRU_WIDTH
    rx_ref[...] = p[:, o_rx:o_ry]
    ry_ref[...] = p[:, o_ry:]
    cos, sin_a, sin_b = cos_ref[...], sina_ref[...], sinb_ref[...]
    qn = _rms(p[:, :o_kv], qn_ref[...]).astype(BF16)
    q = _dot(qn, wuq_ref[...]) * MLA_SCALE
    cn = _rms(p[:, o_kv:o_kr], kvn_ref[...]).astype(BF16)
    kv = _dot(cn, wukv_ref[...])
    kr = _rope(p[:, o_kr:o_rx], cos, sin_a, sin_b).astype(BF16)
    for hd in range(MLA_HEADS):
        qo = hd * MLA_QK
        q_ref[0, hd, :, :MLA_NOPE] = q[:, qo:qo + MLA_NOPE].astype(BF16)
        q_ref[0, hd, :, MLA_NOPE:] = _rope(q[:, qo + MLA_NOPE:qo + MLA_QK], cos, sin_a, sin_b).astype(BF16)
        ko = hd * (MLA_NOPE + MLA_V)
        k_ref[0, hd, :, :MLA_NOPE] = kv[:, ko:ko + MLA_NOPE].astype(BF16)
        k_ref[0, hd, :, MLA_NOPE:] = kr
        v_ref[0, hd] = kv[:, ko + MLA_NOPE:ko + MLA_NOPE + MLA_V].astype(BF16)


def _premix0(x, gain, shift, scale, w_in, q_norm, w_uq, kv_norm, w_ukv, cos, sin_a, sin_b, tm=256):
    b, t, d = x.shape
    n = b * t
    tpb = t // tm
    nmod = shift.shape[0]
    mod_idx = (lambda i: (i // tpb, 0, 0)) if nmod > 1 else (lambda i: (0, 0, 0))
    const2 = lambda i: (0, 0)
    in_cols = w_in.shape[1]
    head_spec = lambda w: pl.BlockSpec((1, MLA_HEADS, tm, w), lambda i: (i // tpb, 0, i % tpb, 0))
    return pl.pallas_call(
        _premix0_kernel,
        grid=(n // tm,),
        in_specs=[
            pl.BlockSpec((tm, d), lambda i: (i, 0)),
            _resident((1, d), const2),
            pl.BlockSpec((1, 1, d), mod_idx),
            pl.BlockSpec((1, 1, d), mod_idx),
            _resident((d, in_cols), const2),
            _resident((1, MLA_Q_LORA), const2),
            _resident(w_uq.shape, const2),
            _resident((1, MLA_KV_LORA), const2),
            _resident(w_ukv.shape, const2),
            pl.BlockSpec((tm, ROPE_PAD), lambda i: (i % tpb, 0)),
            pl.BlockSpec((tm, ROPE_PAD), lambda i: (i % tpb, 0)),
            pl.BlockSpec((tm, ROPE_PAD), lambda i: (i % tpb, 0)),
        ],
        out_specs=[
            head_spec(MLA_QK), head_spec(MLA_QK), head_spec(MLA_V),
            pl.BlockSpec((tm, LRU_WIDTH), lambda i: (i, 0)),
            pl.BlockSpec((tm, LRU_WIDTH), lambda i: (i, 0)),
        ],
        out_shape=[
            jax.ShapeDtypeStruct((b, MLA_HEADS, t, MLA_QK), BF16),
            jax.ShapeDtypeStruct((b, MLA_HEADS, t, MLA_QK), BF16),
            jax.ShapeDtypeStruct((b, MLA_HEADS, t, MLA_V), BF16),
            jax.ShapeDtypeStruct((n, LRU_WIDTH), F32),
            jax.ShapeDtypeStruct((n, LRU_WIDTH), F32),
        ],
        compiler_params=_compiler_params(("parallel",), 48 * 2**20),
        name="premix0",
    )(x.reshape(n, d), gain.reshape(1, d), shift, scale, w_in, q_norm.reshape(1, -1), w_uq,
      kv_norm.reshape(1, -1), w_ukv, cos, sin_a, sin_b)


def _attn_kernel(*refs, n_seg):
    q_ref = refs[0]
    k_refs = refs[1:1 + n_seg]
    v_refs = refs[1 + n_seg:1 + 2 * n_seg]
    o_ref = refs[1 + 2 * n_seg]
    q = q_ref[0, 0]
    s = [_dot_nt(q, k[0, 0]) for k in k_refs]
    m = functools.reduce(jnp.maximum, [jnp.max(x, axis=-1, keepdims=True) for x in s])
    p = [jnp.exp(x - m) for x in s]
    l = functools.reduce(jnp.add, [jnp.sum(x, axis=-1, keepdims=True) for x in p])
    o = functools.reduce(jnp.add, [_dot(x.astype(BF16), v[0, 0]) for x, v in zip(p, v_refs)])
    o_ref[0] = (o / l).astype(o_ref.dtype)


def _mla_attention(q, ks, vs, tq):
    b, h, t, dk = q.shape
    dv = vs[0].shape[-1]
    n_seg = len(ks)
    kv_spec = lambda a: pl.BlockSpec((1, 1) + a.shape[2:], lambda bi, hi, qi: (bi, hi, 0, 0))
    return pl.pallas_call(
        functools.partial(_attn_kernel, n_seg=n_seg),
        grid=(b, h, t // tq),
        in_specs=[pl.BlockSpec((1, 1, tq, dk), lambda bi, hi, qi: (bi, hi, qi, 0))]
        + [kv_spec(a) for a in ks] + [kv_spec(a) for a in vs],
        out_specs=pl.BlockSpec((1, tq, dv), lambda bi, hi, qi: (bi, qi, hi)),
        out_shape=jax.ShapeDtypeStruct((b, t, h * dv), BF16),
        compiler_params=_compiler_params(("parallel", "parallel", "parallel"), 40 * 2**20),
        name="mla_attention",
    )(q, *ks, *vs)


def _scan_tile(a, b, row, reverse):
    for d in (1, 2, 4):
        if reverse:
            keep = row < SUBLANES - d
            shift = SUBLANES - d
        else:
            keep = row >= d
            shift = d
        a_s = jnp.where(keep, pltpu.roll(a, shift, 0), 1.0)
        b_s = jnp.where(keep, pltpu.roll(b, shift, 0), 0.0)
        b = a * b_s + b
        a = a * a_s
    return a, b


def _rglru_kernel(rxl_ref, rxc_ref, ryl_ref, ryc_ref, cw_ref, cb_ref, wg_ref, gab_ref, gxb_ref, lam_ref,
                  ol_ref, oc_ref, xpad, af, bf, ab, bb, *, chunk):
    t = rxl_ref.shape[0]
    tc = rxc_ref.shape[0]
    width = rxl_ref.shape[1]
    cw = cw_ref[...]
    cb = cb_ref[...]
    z = -lam_ref[...]
    sp = jnp.maximum(z, 0.0) + jnp.log1p(jnp.exp(-jnp.abs(z)))
    gab = gab_ref[...]
    gxb = gxb_ref[...]
    pad = SUBLANES

    def coeffs(x_ref, n, off_f, off_b):
        xpad[0:pad, :] = jnp.zeros((pad, width), F32)
        xpad[pad:pad + n, :] = x_ref[...]
        xpad[pad + n:2 * pad + n, :] = jnp.zeros((pad, width), F32)
        for c0 in range(0, n, chunk):
            xc = cb
            for tap in range(LRU_CONV):
                xc = xc + cw[tap:tap + 1] * xpad[pad - 2 + tap + c0:pad - 2 + tap + c0 + chunk, :]
            xb = xc.astype(BF16)
            for blk in range(width // LRU_BLOCK_W):
                sl = slice(blk * LRU_BLOCK_W, (blk + 1) * LRU_BLOCK_W)
                g = _dot(xb[:, sl], wg_ref[blk])
                xs = xc[:, sl]
                for d, (a_ref, b_ref, off) in enumerate(((af, bf, off_f), (ab, bb, off_b))):
                    g0 = 2 * d * LRU_BLOCK_W
                    r = jax.nn.sigmoid(g[:, g0:g0 + LRU_BLOCK_W] + gab[d:d + 1, sl])
                    i = jax.nn.sigmoid(g[:, g0 + LRU_BLOCK_W:g0 + 2 * LRU_BLOCK_W] + gxb[d:d + 1, sl])
                    log_a = -LRU_C * r * sp[d:d + 1, sl]
                    a = jnp.exp(log_a)
                    one_minus_a2 = -jnp.tanh(log_a) * (a * a + 1.0)
                    a_ref[off + c0:off + c0 + chunk, sl] = a
                    b_ref[off + c0:off + c0 + chunk, sl] = jnp.sqrt(one_minus_a2) * (i * xs)

    coeffs(rxc_ref, tc, 0, t)
    coeffs(rxl_ref, t, tc, 0)

    n_tiles = (t + tc) // SUBLANES
    row = lax.broadcasted_iota(jnp.int32, (SUBLANES, width), 0)

    def body(k, carry):
        hf, hb = carry
        r0 = pl.multiple_of(k * SUBLANES, SUBLANES)
        a, b = _scan_tile(af[pl.ds(r0, SUBLANES), :], bf[pl.ds(r0, SUBLANES), :], row, False)
        h = a * hf + b
        bf[pl.ds(r0, SUBLANES), :] = h
        hf = h[SUBLANES - 1:SUBLANES]
        r1 = pl.multiple_of((n_tiles - 1 - k) * SUBLANES, SUBLANES)
        a, b = _scan_tile(ab[pl.ds(r1, SUBLANES), :], bb[pl.ds(r1, SUBLANES), :], row, True)
        h = a * hb + b
        bb[pl.ds(r1, SUBLANES), :] = h
        hb = h[0:1]
        return hf, hb

    zero = jnp.zeros((1, width), F32)
    lax.fori_loop(0, n_tiles, body, (zero, zero), unroll=4)

    for c0 in range(0, t, chunk):
        rec = bf[tc + c0:tc + c0 + chunk, :] + bb[c0:c0 + chunk, :]
        ol_ref[c0:c0 + chunk, :] = (rec * _gelu(ryl_ref[c0:c0 + chunk, :])).astype(ol_ref.dtype)
    for c0 in range(0, tc, chunk):
        rec = bf[c0:c0 + chunk, :] + bb[t + c0:t + c0 + chunk, :]
        oc_ref[c0:c0 + chunk, :] = (rec * _gelu(ryc_ref[c0:c0 + chunk, :])).astype(oc_ref.dtype)


def _rglru(rx_l, rx_c, ry_l, ry_c, batch, conv_w, conv_b, w_gates, ga_b, gx_b, lam, width=256, chunk=256):
    t = rx_l.shape[0] // batch
    tc = rx_c.shape[0] // batch
    nblk = width // LRU_BLOCK_W
    lat = pl.BlockSpec((t, width), lambda bi, ci: (bi, ci))
    ctx = pl.BlockSpec((tc, width), lambda bi, ci: (bi, ci))
    par = lambda rows: pl.BlockSpec((rows, width), lambda bi, ci: (0, ci))
    seq = t + tc
    return pl.pallas_call(
        functools.partial(_rglru_kernel, chunk=chunk),
        grid=(batch, LRU_WIDTH // width),
        in_specs=[lat, ctx, lat, ctx, par(LRU_CONV), par(1),
                  pl.BlockSpec((nblk,) + w_gates.shape[1:], lambda bi, ci: (ci, 0, 0)),
                  par(2), par(2), par(2)],
        out_specs=[lat, ctx],
        out_shape=[jax.ShapeDtypeStruct(rx_l.shape, BF16), jax.ShapeDtypeStruct(rx_c.shape, BF16)],
        scratch_shapes=[pltpu.VMEM((t + 2 * SUBLANES, width), F32)] + [pltpu.VMEM((seq, width), F32)] * 4,
        compiler_params=_compiler_params(("parallel", "parallel"), 40 * 2**20),
        name="rglru",
    )(rx_l, rx_c, ry_l, ry_c, conv_w, conv_b.reshape(1, -1), w_gates, ga_b, gx_b, lam)


def _mixout_kernel(a_ref, b_ref, wa_ref, wb_ref, x_ref, g_ref, o_ref):
    y = _dot(a_ref[...], wa_ref[...]) + _dot(b_ref[...], wb_ref[...])
    o_ref[...] = x_ref[...] + g_ref[0] * y


def _mixout(a, b, a_blk, b_blk, w, x, gate, rows_per_mod, tm=512):
    n, d = x.shape
    kh = w.shape[0] // 2
    nmod = gate.shape[0]
    mod_idx = (lambda i: ((i * tm) // rows_per_mod, 0, 0)) if nmod > 1 else (lambda i: (0, 0, 0))
    return pl.pallas_call(
        _mixout_kernel,
        grid=(n // tm,),
        in_specs=[
            pl.BlockSpec((tm, kh), lambda i: (i, a_blk)),
            pl.BlockSpec((tm, kh), lambda i: (i, b_blk)),
            _resident((kh, d), lambda i: (0, 0)),
            _resident((kh, d), lambda i: (1, 0)),
            pl.BlockSpec((tm, d), lambda i: (i, 0)),
            pl.BlockSpec((1, 1, d), mod_idx),
        ],
        out_specs=pl.BlockSpec((tm, d), lambda i: (i, 0)),
        out_shape=jax.ShapeDtypeStruct((n, d), F32),
        compiler_params=_compiler_params(("parallel",), 48 * 2**20),
        name="mixout",
    )(a, b, w, w, x, gate)


def _ffn_kernel(x_ref, xp_ref, xn_ref, gain_ref, sh_ref, sc_ref, gt_ref, wg_ref, wu_ref, cw_ref, cb_ref,
                wd_ref, fn_ref, o_ref, h_scr, hh_scr, gpad, *, seq_len, final_norm):
    i = pl.program_id(0)
    j = pl.program_id(1)
    tm = x_ref.shape[0]
    halo = SUBLANES

    @pl.when(j == 0)
    def _():
        gain, shift, scale = gain_ref[...], sh_ref[0], sc_ref[0]
        h_scr[...] = _norm_mod(x_ref[...], gain, shift, scale).astype(BF16)
        edge = jnp.concatenate([xp_ref[...], xn_ref[...]], axis=0)
        hh_scr[...] = _norm_mod(edge, gain, shift, scale).astype(BF16)

    h = h_scr[...]
    g = _dot(h, wg_ref[...])
    u = _dot(h, wu_ref[...])
    ge = _dot(hh_scr[...], wg_ref[...])
    gpad[halo - 1:halo, :] = ge[halo - 1:halo]
    gpad[halo:halo + tm, :] = g
    gpad[halo + tm:halo + tm + 1, :] = ge[halo:halo + 1]
    pos = (i * tm + lax.broadcasted_iota(jnp.int32, (tm, 1), 0)) & (seq_len - 1)
    g_prev = jnp.where(pos == 0, 0.0, gpad[halo - 1:halo - 1 + tm, :])
    g_next = jnp.where(pos == seq_len - 1, 0.0, gpad[halo + 1:halo + 1 + tm, :])
    cw = cw_ref[...]
    conv = cb_ref[...] + cw[0:1] * g_prev + cw[1:2] * g + cw[2:3] * g_next
    act = (_gelu(conv) * u).astype(BF16)
    part = _dot(act, wd_ref[...])

    @pl.when(j == 0)
    def _():
        o_ref[...] = part

    @pl.when(j > 0)
    def _():
        o_ref[...] += part

    @pl.when(j == pl.num_programs(1) - 1)
    def _():
        y = x_ref[...] + gt_ref[0] * o_ref[...]
        if final_norm:
            y = _rms(y, fn_ref[...])
        o_ref[...] = y


def _ffn(x, gain, shift, scale, gate, w_up, conv_w, conv_b, w_down, final_gain, seq_len, rows_per_mod,
         final_norm, tm=512, th=512):
    n, d = x.shape
    hidden = w_down.shape[0]
    nj = hidden // th
    assert seq_len & (seq_len - 1) == 0 and (tm % seq_len == 0 or seq_len % tm == 0)
    nmod = gate.shape[0]
    mod_idx = (lambda i, j: ((i * tm) // rows_per_mod, 0, 0)) if nmod > 1 else (lambda i, j: (0, 0, 0))
    bpt = tm // SUBLANES
    last_blk = n // SUBLANES - 1
    const2 = lambda i, j: (0, 0)
    return pl.pallas_call(
        functools.partial(_ffn_kernel, seq_len=seq_len, final_norm=final_norm),
        grid=(n // tm, nj),
        in_specs=[
            pl.BlockSpec((tm, d), lambda i, j: (i, 0)),
            pl.BlockSpec((SUBLANES, d), lambda i, j: (jnp.maximum(i * bpt - 1, 0), 0)),
            pl.BlockSpec((SUBLANES, d), lambda i, j: (jnp.minimum((i + 1) * bpt, last_blk), 0)),
            _resident((1, d), const2),
            pl.BlockSpec((1, 1, d), mod_idx),
            pl.BlockSpec((1, 1, d), mod_idx),
            pl.BlockSpec((1, 1, d), mod_idx),
            pl.BlockSpec((d, th), lambda i, j: (0, j)),
            pl.BlockSpec((d, th), lambda i, j: (0, j + nj)),
            pl.BlockSpec((conv_w.shape[0], th), lambda i, j: (0, j)),
            pl.BlockSpec((1, th), lambda i, j: (0, j)),
            pl.BlockSpec((th, d), lambda i, j: (j, 0)),
            _resident((1, d), const2),
        ],
        out_specs=pl.BlockSpec((tm, d), lambda i, j: (i, 0)),
        out_shape=jax.ShapeDtypeStruct((n, d), F32),
        scratch_shapes=[
            pltpu.VMEM((tm, d), BF16),
            pltpu.VMEM((2 * SUBLANES, d), BF16),
            pltpu.VMEM((tm + 2 * SUBLANES, th), F32),
        ],
        compiler_params=_compiler_params(("parallel", "arbitrary"), 52 * 2**20),
        name="conv_ffn",
    )(x, x, x, gain.reshape(1, d), shift, scale, gate, w_up, w_up, conv_w, conv_b.reshape(1, -1), w_down,
      final_gain.reshape(1, d))


def _premix1_kernel(x_ref, gain_ref, sh_ref, sc_ref, w_ref, o_ref, h_scr, *, n_q_tiles):
    j = pl.program_id(1)

    @pl.when(j == 0)
    def _():
        h_scr[...] = _norm_mod(x_ref[...], gain_ref[...], sh_ref[0], sc_ref[0]).astype(BF16)

    y = _dot(h_scr[...], w_ref[...])
    if n_q_tiles:
        y = y * jnp.where(j < n_q_tiles, NA_SCALE, 1.0)
    o_ref[...] = y.astype(o_ref.dtype)


def _premix1(x, gain, shift, scale, w, rows_per_mod, n_q_cols, tm=1024, tn=1024):
    n, d = x.shape
    cols = w.shape[1]
    nmod = shift.shape[0]
    mod_idx = (lambda i, j: ((i * tm) // rows_per_mod, 0, 0)) if nmod > 1 else (lambda i, j: (0, 0, 0))
    return pl.pallas_call(
        functools.partial(_premix1_kernel, n_q_tiles=n_q_cols // tn),
        grid=(n // tm, cols // tn),
        in_specs=[
            pl.BlockSpec((tm, d), lambda i, j: (i, 0)),
            _resident((1, d), lambda i, j: (0, 0)),
            pl.BlockSpec((1, 1, d), mod_idx),
            pl.BlockSpec((1, 1, d), mod_idx),
            pl.BlockSpec((d, tn), lambda i, j: (0, j)),
        ],
        out_specs=pl.BlockSpec((tm, tn), lambda i, j: (i, j)),
        out_shape=jax.ShapeDtypeStruct((n, cols), BF16),
        scratch_shapes=[pltpu.VMEM((tm, d), BF16)],
        compiler_params=_compiler_params(("parallel", "arbitrary"), 48 * 2**20),
        name="premix1",
    )(x, gain.reshape(1, d), shift, scale, w)


NA_QROWS = 4
NA_BAND = NA_QROWS + NA_ROWS


def _na_kernel(q_ref, k_ref, v_ref, kc_ref, vc_ref, bm_ref, o_ref):
    g = pl.program_id(2)
    n_groups = pl.num_programs(2)
    rows = k_ref.shape[1] // GRID_W
    u0 = jnp.clip(g * NA_QROWS - NA_ROWS // 2, 0, rows - NA_BAND)
    start = pl.multiple_of(u0 * GRID_W, GRID_W)
    pat = jnp.where(g == 0, 0, jnp.where(g == n_groups - 1, 2, 1))
    q = q_ref[0]
    k_band = k_ref[0, pl.ds(start, NA_BAND * GRID_W), :]
    v_band = v_ref[0, pl.ds(start, NA_BAND * GRID_W), :]
    s_b = _dot_nt(q, k_band) + bm_ref[0, pat]
    s_c = _dot_nt(q, kc_ref[0])
    m = jnp.maximum(jnp.max(s_b, axis=-1, keepdims=True), jnp.max(s_c, axis=-1, keepdims=True))
    p_b = jnp.exp(s_b - m)
    p_c = jnp.exp(s_c - m)
    l = jnp.sum(p_b, axis=-1, keepdims=True) + jnp.sum(p_c, axis=-1, keepdims=True)
    o = _dot(p_b.astype(BF16), v_band) + _dot(p_c.astype(BF16), vc_ref[0])
    o_ref[0] = (o / l).astype(o_ref.dtype)


def _na_bias_mask(rel_bias):
    a = jnp.arange(NA_QROWS)[:, None, None, None]
    qc = jnp.arange(GRID_W)[None, :, None, None]
    c = jnp.arange(NA_BAND)[None, None, :, None]
    kc = jnp.arange(GRID_W)[None, None, None, :]
    cs = jnp.clip(qc - NA_COLS // 2, 0, GRID_W - NA_COLS)
    col_ok = (kc >= cs) & (kc < cs + NA_COLS)
    dc = jnp.clip(kc - qc + NA_COLS - 1, 0, 2 * NA_COLS - 2)
    out = []
    for dr_off, lo in ((NA_ROWS - 1, 0 * a), (NA_ROWS - 1 - NA_QROWS, a), (NA_ROWS - 1 - 2 * NA_QROWS, NA_QROWS + 0 * a)):
        row_ok = (c >= lo) & (c < lo + NA_ROWS)
        dr = jnp.clip(c - a + dr_off, 0, 2 * NA_ROWS - 2)
        vals = rel_bias.astype(F32)[:, dr, dc]
        bm = jnp.where(row_ok & col_ok, vals, NEG_INF)
        out.append(bm.reshape(NA_HEADS, NA_QROWS * GRID_W, NA_BAND * GRID_W))
    return jnp.stack(out, axis=1)


def _na_attention(qkv, kv_c, bias_mask, batch):
    t = qkv.shape[0] // batch
    tc = kv_c.shape[0] // batch
    qkv = qkv.reshape(batch, t, 3 * NA_WIDTH)
    kv_c = kv_c.reshape(batch, tc, 2 * NA_WIDTH)
    tq = NA_QROWS * GRID_W
    dh = NA_HEAD_DIM
    out = pl.pallas_call(
        _na_kernel,
        grid=(NA_HEADS, batch, t // tq),
        in_specs=[
            pl.BlockSpec((1, tq, dh), lambda h, b, g: (b, g, h)),
            pl.BlockSpec((1, t, dh), lambda h, b, g: (b, 0, NA_HEADS + h)),
            pl.BlockSpec((1, t, dh), lambda h, b, g: (b, 0, 2 * NA_HEADS + h)),
            pl.BlockSpec((1, tc, dh), lambda h, b, g: (b, 0, h)),
            pl.BlockSpec((1, tc, dh), lambda h, b, g: (b, 0, NA_HEADS + h)),
            pl.BlockSpec((1,) + bias_mask.shape[1:], lambda h, b, g: (h, 0, 0, 0)),
        ],
        out_specs=pl.BlockSpec((1, tq, dh), lambda h, b, g: (b, g, h)),
        out_shape=jax.ShapeDtypeStruct((batch, t, NA_WIDTH), BF16),
        compiler_params=_compiler_params(("parallel", "parallel", "parallel"), 40 * 2**20),
        name="na_attention",
    )(qkv, qkv, qkv, kv_c, kv_c, bias_mask)
    return out.reshape(batch * t, NA_WIDTH)


def _rope_tables(t):
    pos = jnp.arange(t)
    inv = ROPE_THETA ** (-jnp.arange(ROPE_AX_PAIRS, dtype=F32) / ROPE_AX_PAIRS)
    ar = (pos // GRID_W).astype(F32)[:, None] * inv
    ac = (pos % GRID_W).astype(F32)[:, None] * inv
    ang = jnp.concatenate([ar, ar, ac, ac], axis=-1)
    cos, sin = jnp.cos(ang), jnp.sin(ang)
    first_half = (jnp.arange(MLA_ROPE) % (2 * ROPE_AX_PAIRS)) < ROPE_AX_PAIRS
    pad = ((0, 0), (0, ROPE_PAD - MLA_ROPE))
    cos_p = jnp.pad(cos, pad, constant_values=1.0)
    sin_a = jnp.pad(jnp.where(first_half, -sin, 0.0), pad)
    sin_b = jnp.pad(jnp.where(first_half, 0.0, sin), pad)
    return cos_p, sin_a, sin_b


def _identity_rope_tables(t):
    return jnp.ones((t, ROPE_PAD), F32), jnp.zeros((t, ROPE_PAD), F32), jnp.zeros((t, ROPE_PAD), F32)


def kernel(x, c, ctx, c_ctx, mod_w, mod_b, norm_mix, norm_ffn, mla_w_in, mla_q_norm, mla_w_uq, mla_kv_norm, mla_w_ukv, lru_conv_w, lru_conv_b, lru_gate_a_w, lru_gate_a_b, lru_gate_x_w, lru_gate_x_b, lru_lambda, mix_w_out, na_w_qkv, na_rel_bias, na_w_out, ffn_w_up, ffn_conv_w, ffn_conv_b, ffn_w_down, final_norm):
    batch, t, d = x.shape
    tc = ctx.shape[1]
    n_lat, n_ctx = batch * t, batch * tc

    cond_rows = 2 * SUBLANES
    cond = jnp.zeros((cond_rows, d), F32).at[:batch].set(c).at[batch].set(c_ctx)
    mod = _modulation(cond, mod_w, mod_b)

    def mod_parts(layer):
        lat = [mod[layer, :batch, k * d:(k + 1) * d].reshape(batch, 1, d) for k in range(6)]
        cx = [mod[layer, batch:batch + 1, k * d:(k + 1) * d].reshape(1, 1, d) for k in range(6)]
        return lat, cx

    x2 = x.reshape(n_lat, d)
    c2 = ctx.reshape(n_ctx, d)

    (sh1, sc1, g1, sh2, sc2, g2), (csh1, csc1, cg1, csh2, csc2, cg2) = mod_parts(0)
    w_in = mla_w_in[0]
    o_kr = MLA_Q_LORA + MLA_KV_LORA + MLA_ROPE
    w_in_p = jnp.concatenate(
        [w_in[:, :o_kr], jnp.zeros((d, ROPE_PAD - MLA_ROPE), w_in.dtype), w_in[:, o_kr:]], axis=1).astype(BF16)
    w_uq_p = jnp.pad(mla_w_uq[0].reshape(MLA_Q_LORA, MLA_HEADS, MLA_NOPE + MLA_ROPE),
                     ((0, 0), (0, 0), (0, ROPE_PAD - MLA_ROPE))).reshape(MLA_Q_LORA, MLA_HEADS * MLA_QK).astype(BF16)
    w_ukv = mla_w_ukv[0].astype(BF16)
    w_gates = jnp.concatenate([lru_gate_a_w[0, 0], lru_gate_x_w[0, 0], lru_gate_a_w[0, 1], lru_gate_x_w[0, 1]],
                              axis=-1).astype(BF16)
    w_mix = mix_w_out[0].astype(BF16)

    pm = functools.partial(_premix0, gain=norm_mix[0], w_in=w_in_p, q_norm=mla_q_norm[0], w_uq=w_uq_p,
                           kv_norm=mla_kv_norm[0], w_ukv=w_ukv)
    cos, sin_a, sin_b = _rope_tables(t)
    q_l, k_l, v_l, rx_l, ry_l = pm(x, shift=sh1, scale=sc1, cos=cos, sin_a=sin_a, sin_b=sin_b)
    cos, sin_a, sin_b = _identity_rope_tables(tc)
    q_c, k_c, v_c, rx_c, ry_c = pm(ctx, shift=csh1, scale=csc1, cos=cos, sin_a=sin_a, sin_b=sin_b)

    att_l = _mla_attention(q_l, [k_c, k_l], [v_c, v_l], tq=512).reshape(n_lat, -1)
    att_c = _mla_attention(q_c, [k_c], [v_c], tq=tc).reshape(n_ctx, -1)
    rec_l, rec_c = _rglru(rx_l, rx_c, ry_l, ry_c, batch, lru_conv_w[0], lru_conv_b[0], w_gates,
                          lru_gate_a_b[0], lru_gate_x_b[0], lru_lambda[0])

    x2 = _mixout(att_l, rec_l, 0, 0, w_mix, x2, g1, t)
    c2 = _mixout(att_c, rec_c, 0, 0, w_mix, c2, cg1, tc)

    ffn0 = functools.partial(_ffn, gain=norm_ffn[0], w_up=ffn_w_up[0].astype(BF16), conv_w=ffn_conv_w[0],
                             conv_b=ffn_conv_b[0], w_down=ffn_w_down[0].astype(BF16), final_gain=final_norm,
                             final_norm=False)
    x2 = ffn0(x2, shift=sh2, scale=sc2, gate=g2, seq_len=t, rows_per_mod=t)
    c2 = ffn0(c2, shift=csh2, scale=csc2, gate=cg2, seq_len=tc, rows_per_mod=tc)

    (sh1, sc1, g1, sh2, sc2, g2), (csh1, csc1, _, _, _, _) = mod_parts(1)
    w_qkv = na_w_qkv[0].astype(BF16)
    qkv = _premix1(x2, norm_mix[1], sh1, sc1, w_qkv, t, NA_WIDTH)
    kv_c = _premix1(c2, norm_mix[1], csh1, csc1, w_qkv[:, NA_WIDTH:], tc, 0)
    o = _na_attention(qkv, kv_c, _na_bias_mask(na_rel_bias[0]), batch)
    x2 = _mixout(o, o, 0, 1, na_w_out[0].astype(BF16), x2, g1, t)
    x2 = _ffn(x2, norm_ffn[1], sh2, sc2, g2, ffn_w_up[1].astype(BF16), ffn_conv_w[1], ffn_conv_b[1],
              ffn_w_down[1].astype(BF16), final_norm, seq_len=t, rows_per_mod=t, final_norm=True)
    return x2.reshape(batch, t, d)
```

```python
import functools

import numpy as np
import jax
import jax.numpy as jnp
from jax import lax
from jax.experimental import pallas as pl
from jax.experimental.pallas import tpu as pltpu

D_MODEL = 2048
DEPTH = 2
GRID_W = 64
NORM_EPS = 1e-6
NEG_INF = -1e30

MLA_HEADS = D_MODEL // 256
MLA_Q_LORA = D_MODEL // 4
MLA_KV_LORA = D_MODEL // 8
MLA_NOPE = 128
MLA_ROPE = 64
MLA_V = 128
MLA_SCALE = (MLA_NOPE + MLA_ROPE) ** -0.5
ROPE_THETA = 10000.0
ROPE_AX_PAIRS = MLA_ROPE // 4

LRU_WIDTH = D_MODEL // 2
LRU_BLOCKS = 8
LRU_BLOCK_W = LRU_WIDTH // LRU_BLOCKS
LRU_CONV = 4
LRU_C = 8.0

NA_HEADS = 16
NA_HEAD_DIM = D_MODEL // NA_HEADS
NA_WIDTH = NA_HEADS * NA_HEAD_DIM
NA_SCALE = NA_HEAD_DIM ** -0.5
NA_ROWS = 8
NA_COLS = 16

FFN_HIDDEN = (D_MODEL * 11) // 4

LANES = 128
SUBLANES = 8
VMEM_BYTES = 64 * 1024 * 1024
ROPE_PAD = LANES
MLA_QK = MLA_NOPE + ROPE_PAD

BF16 = jnp.bfloat16
F32 = jnp.float32


def _compiler_params(semantics, vmem_bytes):
    limit = min(int(vmem_bytes), VMEM_BYTES - 4 * 1024 * 1024)
    return pltpu.CompilerParams(dimension_semantics=semantics, vmem_limit_bytes=limit)


def _resident(shape, index_map):
    return pl.BlockSpec(shape, index_map, pipeline_mode=pl.Buffered(1))


def _rms(x, g):
    return x * lax.rsqrt(jnp.mean(x * x, axis=-1, keepdims=True) + NORM_EPS) * g


def _norm_mod(x, gain, shift, scale):
    return _rms(x, gain) * (1.0 + scale) + shift


def _gelu(x):
    return 0.5 * x * (1.0 + jnp.tanh(0.7978845608028654 * (x + 0.044715 * (x * x * x))))


def _dot(a, b):
    return jnp.dot(a, b, preferred_element_type=F32)


def _dot_nt(a, b):
    return lax.dot_general(a, b, (((1,), (1,)), ((), ())), preferred_element_type=F32)


def _dot_tn(a, b):
    return lax.dot_general(a, b, (((0,), (0,)), ((), ())), preferred_element_type=F32)


def _mod_kernel(c_ref, w_ref, b_ref, o_ref):
    c = c_ref[...]
    s = (c * jax.nn.sigmoid(c)).astype(BF16)
    o_ref[0] = _dot(s, w_ref[0].astype(BF16)) + b_ref[0]


def _modulation(cond, mod_w, mod_b, tn=1024):
    rows = cond.shape[0]
    depth, d, n = mod_w.shape
    return pl.pallas_call(
        _mod_kernel,
        grid=(depth, n // tn),
        in_specs=[
            pl.BlockSpec((rows, d), lambda l, j: (0, 0)),
            pl.BlockSpec((1, d, tn), lambda l, j: (l, 0, j)),
            pl.BlockSpec((1, 1, tn), lambda l, j: (l, 0, j)),
        ],
        out_specs=pl.BlockSpec((1, rows, tn), lambda l, j: (l, 0, j)),
        out_shape=jax.ShapeDtypeStruct((depth, rows, n), F32),
        compiler_params=_compiler_params(("parallel", "parallel"), 40 * 2**20),
        name="modulation",
    )(cond, mod_w, mod_b.reshape(depth, 1, n))


def _rope(x, cos, sin_a, sin_b):
    return x * cos + pltpu.roll(x, LANES - ROPE_AX_PAIRS, 1) * sin_a + pltpu.roll(x, ROPE_AX_PAIRS, 1) * sin_b


def _premix0_kernel(x_ref, gain_ref, sh_ref, sc_ref, win_ref, qn_ref, wuq_ref, kvn_ref, wukv_ref,
                    cos_ref, sina_ref, sinb_ref, q_ref, k_ref, v_ref, rx_ref, ry_ref):
    h = _norm_mod(x_ref[...], gain_ref[...], sh_ref[0], sc_ref[0]).astype(BF16)
    p = _dot(h, win_ref[...])
    o_kv = MLA_Q_LORA
    o_kr = o_kv + MLA_KV_LORA
    o_rx = o_kr + ROPE_PAD
    o_ry = o_rx + LRU_WIDTH
    rx_ref[...] = p[:, o_rx:o_ry]
    ry_ref[...] = p[:, o_ry:]
    cos, sin_a, sin_b = cos_ref[...], sina_ref[...], sinb_ref[...]
    qn = _rms(p[:, :o_kv], qn_ref[...]).astype(BF16)
    q = _dot(qn, wuq_ref[...]) * MLA_SCALE
    cn = _rms(p[:, o_kv:o_kr], kvn_ref[...]).astype(BF16)
    kv = _dot(cn, wukv_ref[...])
    kr = _rope(p[:, o_kr:o_rx], cos, sin_a, sin_b).astype(BF16)
    for hd in range(MLA_HEADS):
        qo = hd * MLA_QK
        q_ref[0, hd, :, :MLA_NOPE] = q[:, qo:qo + MLA_NOPE].astype(BF16)
        q_ref[0, hd, :, MLA_NOPE:] = _rope(q[:, qo + MLA_NOPE:qo + MLA_QK], cos, sin_a, sin_b).astype(BF16)
        ko = hd * (MLA_NOPE + MLA_V)
        k_ref[0, hd, :, :MLA_NOPE] = kv[:, ko:ko + MLA_NOPE].astype(BF16)
        k_ref[0, hd, :, MLA_NOPE:] = kr
        v_ref[0, hd] = kv[:, ko + MLA_NOPE:ko + MLA_NOPE + MLA_V].astype(BF16)


def _premix0(x, gain, shift, scale, w_in, q_norm, w_uq, kv_norm, w_ukv, cos, sin_a, sin_b, tm=256):
    b, t, d = x.shape
    n = b * t
    tpb = t // tm
    nmod = shift.shape[0]
    mod_idx = (lambda i: (i // tpb, 0, 0)) if nmod > 1 else (lambda i: (0, 0, 0))
    const2 = lambda i: (0, 0)
    in_cols = w_in.shape[1]
    head_spec = lambda w: pl.BlockSpec((1, MLA_HEADS, tm, w), lambda i: (i // tpb, 0, i % tpb, 0))
    return pl.pallas_call(
        _premix0_kernel,
        grid=(n // tm,),
        in_specs=[
            pl.BlockSpec((tm, d), lambda i: (i, 0)),
            _resident((1, d), const2),
            pl.BlockSpec((1, 1, d), mod_idx),
            pl.BlockSpec((1, 1, d), mod_idx),
            _resident((d, in_cols), const2),
            _resident((1, MLA_Q_LORA), const2),
            _resident(w_uq.shape, const2),
            _resident((1, MLA_KV_LORA), const2),
            _resident(w_ukv.shape, const2),
            pl.BlockSpec((tm, ROPE_PAD), lambda i: (i % tpb, 0)),
            pl.BlockSpec((tm, ROPE_PAD), lambda i: (i % tpb, 0)),
            pl.BlockSpec((tm, ROPE_PAD), lambda i: (i % tpb, 0)),
        ],
        out_specs=[
            head_spec(MLA_QK), head_spec(MLA_QK), head_spec(MLA_V),
            pl.BlockSpec((tm, LRU_WIDTH), lambda i: (i, 0)),
            pl.BlockSpec((tm, LRU_WIDTH), lambda i: (i, 0)),
        ],
        out_shape=[
            jax.ShapeDtypeStruct((b, MLA_HEADS, t, MLA_QK), BF16),
            jax.ShapeDtypeStruct((b, MLA_HEADS, t, MLA_QK), BF16),
            jax.ShapeDtypeStruct((b, MLA_HEADS, t, MLA_V), BF16),
            jax.ShapeDtypeStruct((n, LRU_WIDTH), F32),
            jax.ShapeDtypeStruct((n, LRU_WIDTH), F32),
        ],
        compiler_params=_compiler_params(("parallel",), 48 * 2**20),
        name="premix0",
    )(x.reshape(n, d), gain.reshape(1, d), shift, scale, w_in, q_norm.reshape(1, -1), w_uq,
      kv_norm.reshape(1, -1), w_ukv, cos, sin_a, sin_b)


def _softmax_pv_t(scores_t, values):
    m = functools.reduce(jnp.maximum, [jnp.max(x, axis=0, keepdims=True) for x in scores_t])
    p = [jnp.exp(x - m) for x in scores_t]
    l = functools.reduce(jnp.add, [jnp.sum(x, axis=0, keepdims=True) for x in p])
    o_t = functools.reduce(jnp.add, [_dot_tn(v, x.astype(BF16)) for x, v in zip(p, values)])
    return (o_t / l).T


def _attn_kernel(*refs, n_seg, tq):
    q_ref = refs[0]
    k_refs = refs[1:1 + n_seg]
    v_refs = refs[1 + n_seg:1 + 2 * n_seg]
    o_ref = refs[1 + 2 * n_seg]
    for r0 in range(0, q_ref.shape[2], tq):
        q = q_ref[0, 0, r0:r0 + tq, :]
        s_t = [_dot_nt(k[0, 0], q) for k in k_refs]
        o_ref[0, r0:r0 + tq, :] = _softmax_pv_t(s_t, [v[0, 0] for v in v_refs]).astype(o_ref.dtype)


def _mla_attention(q, ks, vs, tq, n_unit):
    b, h, t, dk = q.shape
    dv = vs[0].shape[-1]
    n_seg = len(ks)
    tb = tq * n_unit
    kv_spec = lambda a: pl.BlockSpec((1, 1) + a.shape[2:], lambda bi, hi, qi: (bi, hi, 0, 0))
    return pl.pallas_call(
        functools.partial(_attn_kernel, n_seg=n_seg, tq=tq),
        grid=(b, h, t // tb),
        in_specs=[pl.BlockSpec((1, 1, tb, dk), lambda bi, hi, qi: (bi, hi, qi, 0))]
        + [kv_spec(a) for a in ks] + [kv_spec(a) for a in vs],
        out_specs=pl.BlockSpec((1, tb, dv), lambda bi, hi, qi: (bi, qi, hi)),
        out_shape=jax.ShapeDtypeStruct((b, t, h * dv), BF16),
        compiler_params=_compiler_params(("parallel", "parallel", "parallel"), 40 * 2**20),
        name="mla_attention",
    )(q, *ks, *vs)


def _scan_tile(a, b, row, reverse):
    for d in (1, 2, 4):
        if reverse:
            keep = row < SUBLANES - d
            shift = SUBLANES - d
        else:
            keep = row >= d
            shift = d
        a_s = jnp.where(keep, pltpu.roll(a, shift, 0), 1.0)
        b_s = jnp.where(keep, pltpu.roll(b, shift, 0), 0.0)
        b = a * b_s + b
        a = a * a_s
    return a, b


def _rglru_kernel(rxl_ref, rxc_ref, ryl_ref, ryc_ref, cw_ref, cb_ref, wg_ref, gab_ref, gxb_ref, lam_ref,
                  ol_ref, oc_ref, xpad, af, bf, ab, bb, *, chunk):
    t = rxl_ref.shape[0]
    tc = rxc_ref.shape[0]
    width = rxl_ref.shape[1]
    cw = cw_ref[...]
    cb = cb_ref[...]
    z = -lam_ref[...]
    sp = jnp.maximum(z, 0.0) + jnp.log1p(jnp.exp(-jnp.abs(z)))
    gab = gab_ref[...]
    gxb = gxb_ref[...]
    pad = SUBLANES

    def coeffs(x_ref, n, off_f, off_b):
        xpad[0:pad, :] = jnp.zeros((pad, width), F32)
        xpad[pad:pad + n, :] = x_ref[...]
        xpad[pad + n:2 * pad + n, :] = jnp.zeros((pad, width), F32)
        for c0 in range(0, n, chunk):
            xc = cb
            for tap in range(LRU_CONV):
                xc = xc + cw[tap:tap + 1] * xpad[pad - 2 + tap + c0:pad - 2 + tap + c0 + chunk, :]
            xb = xc.astype(BF16)
            for blk in range(width // LRU_BLOCK_W):
                sl = slice(blk * LRU_BLOCK_W, (blk + 1) * LRU_BLOCK_W)
                g = _dot(xb[:, sl], wg_ref[blk])
                xs = xc[:, sl]
                for d, (a_ref, b_ref, off) in enumerate(((af, bf, off_f), (ab, bb, off_b))):
                    g0 = 2 * d * LRU_BLOCK_W
                    r = jax.nn.sigmoid(g[:, g0:g0 + LRU_BLOCK_W] + gab[d:d + 1, sl])
                    i = jax.nn.sigmoid(g[:, g0 + LRU_BLOCK_W:g0 + 2 * LRU_BLOCK_W] + gxb[d:d + 1, sl])
                    log_a = -LRU_C * r * sp[d:d + 1, sl]
                    a = jnp.exp(log_a)
                    one_minus_a2 = -jnp.tanh(log_a) * (a * a + 1.0)
                    a_ref[off + c0:off + c0 + chunk, sl] = a
                    b_ref[off + c0:off + c0 + chunk, sl] = jnp.sqrt(one_minus_a2) * (i * xs)

    coeffs(rxc_ref, tc, 0, t)
    coeffs(rxl_ref, t, tc, 0)

    n_tiles = (t + tc) // SUBLANES
    row = lax.broadcasted_iota(jnp.int32, (SUBLANES, width), 0)

    def body(k, carry):
        hf, hb = carry
        r0 = pl.multiple_of(k * SUBLANES, SUBLANES)
        a, b = _scan_tile(af[pl.ds(r0, SUBLANES), :], bf[pl.ds(r0, SUBLANES), :], row, False)
        h = a * hf + b
        bf[pl.ds(r0, SUBLANES), :] = h
        hf = h[SUBLANES - 1:SUBLANES]
        r1 = pl.multiple_of((n_tiles - 1 - k) * SUBLANES, SUBLANES)
        a, b = _scan_tile(ab[pl.ds(r1, SUBLANES), :], bb[pl.ds(r1, SUBLANES), :], row, True)
        h = a * hb + b
        bb[pl.ds(r1, SUBLANES), :] = h
        hb = h[0:1]
        return hf, hb

    zero = jnp.zeros((1, width), F32)
    lax.fori_loop(0, n_tiles, body, (zero, zero), unroll=4)

    for c0 in range(0, t, chunk):
        rec = bf[tc + c0:tc + c0 + chunk, :] + bb[c0:c0 + chunk, :]
        ol_ref[c0:c0 + chunk, :] = (rec * _gelu(ryl_ref[c0:c0 + chunk, :])).astype(ol_ref.dtype)
    for c0 in range(0, tc, chunk):
        rec = bf[c0:c0 + chunk, :] + bb[t + c0:t + c0 + chunk, :]
        oc_ref[c0:c0 + chunk, :] = (rec * _gelu(ryc_ref[c0:c0 + chunk, :])).astype(oc_ref.dtype)


def _rglru(rx_l, rx_c, ry_l, ry_c, batch, conv_w, conv_b, w_gates, ga_b, gx_b, lam, width=256, chunk=256):
    t = rx_l.shape[0] // batch
    tc = rx_c.shape[0] // batch
    nblk = width // LRU_BLOCK_W
    lat = pl.BlockSpec((t, width), lambda bi, ci: (bi, ci))
    ctx = pl.BlockSpec((tc, width), lambda bi, ci: (bi, ci))
    par = lambda rows: pl.BlockSpec((rows, width), lambda bi, ci: (0, ci))
    seq = t + tc
    return pl.pallas_call(
        functools.partial(_rglru_kernel, chunk=chunk),
        grid=(batch, LRU_WIDTH // width),
        in_specs=[lat, ctx, lat, ctx, par(LRU_CONV), par(1),
                  pl.BlockSpec((nblk,) + w_gates.shape[1:], lambda bi, ci: (ci, 0, 0)),
                  par(2), par(2), par(2)],
        out_specs=[lat, ctx],
        out_shape=[jax.ShapeDtypeStruct(rx_l.shape, BF16), jax.ShapeDtypeStruct(rx_c.shape, BF16)],
        scratch_shapes=[pltpu.VMEM((t + 2 * SUBLANES, width), F32)] + [pltpu.VMEM((seq, width), F32)] * 4,
        compiler_params=_compiler_params(("parallel", "parallel"), 40 * 2**20),
        name="rglru",
    )(rx_l, rx_c, ry_l, ry_c, conv_w, conv_b.reshape(1, -1), w_gates, ga_b, gx_b, lam)


def _mixout_kernel(a_ref, b_ref, wa_ref, wb_ref, x_ref, g_ref, o_ref):
    y = _dot(a_ref[...], wa_ref[...]) + _dot(b_ref[...], wb_ref[...])
    o_ref[...] = x_ref[...] + g_ref[0] * y


def _mixout(a, b, a_blk, b_blk, w, x, gate, rows_per_mod, tm=512):
    n, d = x.shape
    kh = w.shape[0] // 2
    nmod = gate.shape[0]
    mod_idx = (lambda i: ((i * tm) // rows_per_mod, 0, 0)) if nmod > 1 else (lambda i: (0, 0, 0))
    return pl.pallas_call(
        _mixout_kernel,
        grid=(n // tm,),
        in_specs=[
            pl.BlockSpec((tm, kh), lambda i: (i, a_blk)),
            pl.BlockSpec((tm, kh), lambda i: (i, b_blk)),
            _resident((kh, d), lambda i: (0, 0)),
            _resident((kh, d), lambda i: (1, 0)),
            pl.BlockSpec((tm, d), lambda i: (i, 0)),
            pl.BlockSpec((1, 1, d), mod_idx),
        ],
        out_specs=pl.BlockSpec((tm, d), lambda i: (i, 0)),
        out_shape=jax.ShapeDtypeStruct((n, d), F32),
        compiler_params=_compiler_params(("parallel",), 48 * 2**20),
        name="mixout",
    )(a, b, w, w, x, gate)


def _ffn_kernel(x_ref, xp_ref, xn_ref, xc_ref, gain_ref, sh_ref, sc_ref, gt_ref, wg_ref, wu_ref, cw_ref, cb_ref,
                wd_ref, fn_ref, o_ref, h_scr, hh_scr, gpad, act_scr, *y_scr, seq_len, nj, nd):
    i = pl.program_id(0)
    j = pl.program_id(1)
    tm = x_ref.shape[0]
    th = act_scr.shape[2]
    td = wd_ref.shape[1]
    halo = SUBLANES

    @pl.when(j == 0)
    def _():
        gain, shift, scale = gain_ref[...], sh_ref[0], sc_ref[0]
        h_scr[...] = _norm_mod(x_ref[...], gain, shift, scale).astype(BF16)
        edge = jnp.concatenate([xp_ref[...], xn_ref[...]], axis=0)
        hh_scr[...] = _norm_mod(edge, gain, shift, scale).astype(BF16)

    @pl.when(j < nj)
    def _():
        h = h_scr[...]
        g = _dot(h, wg_ref[...])
        ge = _dot(hh_scr[...], wg_ref[...])
        gpad[halo - 1:halo, :] = ge[halo - 1:halo]
        gpad[halo:halo + tm, :] = g
        gpad[halo + tm:halo + tm + 1, :] = ge[halo:halo + 1]
        pos = (i * tm + lax.broadcasted_iota(jnp.int32, (tm, 1), 0)) & (seq_len - 1)
        g_prev = jnp.where(pos == 0, 0.0, gpad[halo - 1:halo - 1 + tm, :])
        g_next = jnp.where(pos == seq_len - 1, 0.0, gpad[halo + 1:halo + 1 + tm, :])
        cw = cw_ref[...]
        conv = cb_ref[...] + cw[0:1] * g_prev + cw[1:2] * g + cw[2:3] * g_next
        act_scr[j] = (_gelu(conv) * _dot(h, wu_ref[...])).astype(BF16)

    @pl.when(j >= nj)
    def _():
        gate = gt_ref[0]
        for r0 in range(0, tm, tm // 2):
            rs = slice(r0, r0 + tm // 2)
            acc = _dot(act_scr[0, rs, :], wd_ref[0:th, :])
            for k in range(1, nj):
                acc = acc + _dot(act_scr[k, rs, :], wd_ref[k * th:(k + 1) * th, :])
            y = xc_ref[rs, :] + gate * acc
            if y_scr:
                y_scr[0][j - nj, rs, :] = y
            else:
                o_ref[rs, :] = y

    if y_scr:
        @pl.when(j == nj + nd - 1)
        def _():
            ys = y_scr[0]
            ssq = functools.reduce(jnp.add, [jnp.sum(ys[k] * ys[k], axis=-1, keepdims=True) for k in range(nd)])
            inv = lax.rsqrt(ssq / (nd * td) + NORM_EPS)
            for k in range(nd):
                o_ref[:, k * td:(k + 1) * td] = ys[k] * inv * fn_ref[:, k * td:(k + 1) * td]


def _ffn(x, gain, shift, scale, gate, w_up, conv_w, conv_b, w_down, final_gain, seq_len, rows_per_mod,
         tm=512, th=512, td=512):
    n, d = x.shape
    hidden = w_down.shape[0]
    nj = hidden // th
    nd = d // td
    final_norm = final_gain is not None
    assert seq_len & (seq_len - 1) == 0 and (tm % seq_len == 0 or seq_len % tm == 0)
    nmod = gate.shape[0]
    mod_row = (lambda i: (i * tm) // rows_per_mod) if nmod > 1 else (lambda i: 0)
    mod_idx = lambda i, j: (mod_row(i), 0, 0)
    bpt = tm // SUBLANES
    last_blk = n // SUBLANES - 1
    const2 = lambda i, j: (0, 0)
    up_col = lambda j: jnp.minimum(j, nj - 1)
    down_col = lambda j: jnp.maximum(j - nj, 0)
    if final_norm:
        out_spec = pl.BlockSpec((tm, d), lambda i, j: (i, 0))
        y_scratch = [pltpu.VMEM((nd, tm, td), F32)]
    else:
        out_spec = pl.BlockSpec((tm, td), lambda i, j: (i, down_col(j)))
        y_scratch = []
        final_gain = gain
    return pl.pallas_call(
        functools.partial(_ffn_kernel, seq_len=seq_len, nj=nj, nd=nd),
        grid=(n // tm, nj + nd),
        in_specs=[
            pl.BlockSpec((tm, d), lambda i, j: (i, 0)),
            pl.BlockSpec((SUBLANES, d), lambda i, j: (jnp.maximum(i * bpt - 1, 0), 0)),
            pl.BlockSpec((SUBLANES, d), lambda i, j: (jnp.minimum((i + 1) * bpt, last_blk), 0)),
            pl.BlockSpec((tm, td), lambda i, j: (i, down_col(j))),
            _resident((1, d), const2),
            pl.BlockSpec((1, 1, d), mod_idx),
            pl.BlockSpec((1, 1, d), mod_idx),
            pl.BlockSpec((1, 1, td), lambda i, j: (mod_row(i), 0, down_col(j))),
            pl.BlockSpec((d, th), lambda i, j: (0, up_col(j))),
            pl.BlockSpec((d, th), lambda i, j: (0, up_col(j) + nj)),
            pl.BlockSpec((conv_w.shape[0], th), lambda i, j: (0, up_col(j))),
            pl.BlockSpec((1, th), lambda i, j: (0, up_col(j))),
            pl.BlockSpec((hidden, td), lambda i, j: (0, down_col(j))),
            _resident((1, d), const2),
        ],
        out_specs=out_spec,
        out_shape=jax.ShapeDtypeStruct((n, d), F32),
        scratch_shapes=[
            pltpu.VMEM((tm, d), BF16),
            pltpu.VMEM((2 * SUBLANES, d), BF16),
            pltpu.VMEM((tm + 2 * SUBLANES, th), F32),
            pltpu.VMEM((nj, tm, th), BF16),
        ] + y_scratch,
        compiler_params=_compiler_params(("parallel", "arbitrary"), 56 * 2**20),
        name="conv_ffn",
    )(x, x, x, x, gain.reshape(1, d), shift, scale, gate, w_up, w_up, conv_w, conv_b.reshape(1, -1), w_down,
      final_gain.reshape(1, d))


def _premix1_kernel(x_ref, gain_ref, sh_ref, sc_ref, w_ref, o_ref, h_scr, *, n_q_tiles):
    j = pl.program_id(1)

    @pl.when(j == 0)
    def _():
        h_scr[...] = _norm_mod(x_ref[...], gain_ref[...], sh_ref[0], sc_ref[0]).astype(BF16)

    y = _dot(h_scr[...], w_ref[...])
    if n_q_tiles:
        y = y * jnp.where(j < n_q_tiles, NA_SCALE, 1.0)
    o_ref[...] = y.astype(o_ref.dtype)


def _premix1(x, gain, shift, scale, w, rows_per_mod, n_q_cols, col0=0, tm=1024, tn=1024):
    n, d = x.shape
    cols = w.shape[1] - col0
    j0 = col0 // tn
    nmod = shift.shape[0]
    mod_idx = (lambda i, j: ((i * tm) // rows_per_mod, 0, 0)) if nmod > 1 else (lambda i, j: (0, 0, 0))
    return pl.pallas_call(
        functools.partial(_premix1_kernel, n_q_tiles=n_q_cols // tn),
        grid=(n // tm, cols // tn),
        in_specs=[
            pl.BlockSpec((tm, d), lambda i, j: (i, 0)),
            _resident((1, d), lambda i, j: (0, 0)),
            pl.BlockSpec((1, 1, d), mod_idx),
            pl.BlockSpec((1, 1, d), mod_idx),
            pl.BlockSpec((d, tn), lambda i, j: (0, j + j0)),
        ],
        out_specs=pl.BlockSpec((tm, tn), lambda i, j: (i, j)),
        out_shape=jax.ShapeDtypeStruct((n, cols), BF16),
        scratch_shapes=[pltpu.VMEM((tm, d), BF16)],
        compiler_params=_compiler_params(("parallel", "arbitrary"), 48 * 2**20),
        name="premix1",
    )(x, gain.reshape(1, d), shift, scale, w)


NA_QROWS = 8
NA_BAND = NA_QROWS + NA_ROWS
NA_HEADS_PER_STEP = 1
NA_GROUPS_PER_STEP = 2


def _na_kernel(q_ref, k_ref, v_ref, kc_ref, vc_ref, bm_ref, o_ref):
    rows = k_ref.shape[1] // GRID_W
    n_groups = rows // NA_QROWS
    tq = NA_QROWS * GRID_W
    for gg in range(NA_GROUPS_PER_STEP):
        g = pl.program_id(2) * NA_GROUPS_PER_STEP + gg
        u0 = jnp.clip(g * NA_QROWS - NA_ROWS // 2, 0, rows - NA_BAND)
        start = pl.multiple_of(u0 * GRID_W, GRID_W)
        pat = jnp.where(g == 0, 0, jnp.where(g == n_groups - 1, 2, 1))
        qs = slice(gg * tq, (gg + 1) * tq)
        for hh in range(NA_HEADS_PER_STEP):
            hs = slice(hh * NA_HEAD_DIM, (hh + 1) * NA_HEAD_DIM)
            q = q_ref[0, qs, hs]
            k_band = k_ref[0, pl.ds(start, NA_BAND * GRID_W), hs]
            v_band = v_ref[0, pl.ds(start, NA_BAND * GRID_W), hs]
            s_b = _dot_nt(k_band, q) + bm_ref[hh, pat]
            s_c = _dot_nt(kc_ref[0, :, hs], q)
            o_ref[0, qs, hs] = _softmax_pv_t([s_b, s_c], [v_band, vc_ref[0, :, hs]]).astype(o_ref.dtype)


def _na_bias_mask(rel_bias, rows):
    heads, n_dr, n_dc = rel_bias.shape
    qc = np.arange(GRID_W)[:, None]
    kc = np.arange(GRID_W)[None, :]
    cs = np.clip(qc - NA_COLS // 2, 0, GRID_W - NA_COLS)
    col_ok = (kc >= cs) & (kc < cs + NA_COLS)
    lead = GRID_W - NA_COLS
    ext = jnp.pad(rel_bias.astype(F32), ((0, 0), (0, 0), (lead, lead)))
    width = 2 * GRID_W - 1
    skew = jnp.broadcast_to(ext[:, :, None, :], (heads, n_dr, GRID_W, width))
    skew = jnp.pad(skew, ((0, 0), (0, 0), (0, 0), (0, 1))).reshape(heads, n_dr, GRID_W * (width + 1))
    skew = skew[:, :, :GRID_W * width].reshape(heads, n_dr, GRID_W, width)
    toep = skew[..., GRID_W - 1:]
    toep = jnp.where(col_ok, toep, NEG_INF).transpose(0, 1, 3, 2)
    n_groups = rows // NA_QROWS
    a = np.arange(NA_QROWS)[None, :]
    offs, los = [], []
    for g in (0, 1, n_groups - 1):
        r0 = g * NA_QROWS
        u0 = np.clip(r0 - NA_ROWS // 2, 0, rows - NA_BAND)
        los.append(np.clip(r0 + a - NA_ROWS // 2, 0, rows - NA_ROWS) - u0)
        offs.append(int(u0) - r0 + NA_ROWS - 1)
    assert (n_groups - 2) * NA_QROWS - NA_ROWS // 2 <= rows - NA_BAND
    pad_lo = max(0, NA_QROWS - 1 - min(offs))
    pad_hi = max(0, max(offs) + NA_BAND - n_dr)
    toep = jnp.pad(toep, ((0, 0), (pad_lo, pad_hi), (0, 0), (0, 0)))
    c = np.arange(NA_BAND)[:, None]
    out = []
    for off, lo in zip(offs, los):
        row_ok = np.repeat(np.repeat((c >= lo) & (c < lo + NA_ROWS), GRID_W, axis=0), GRID_W, axis=1)
        slabs = [toep[:, off - ai + pad_lo:off - ai + pad_lo + NA_BAND].reshape(heads, NA_BAND * GRID_W, GRID_W)
                 for ai in range(NA_QROWS)]
        out.append(jnp.where(row_ok, jnp.concatenate(slabs, axis=-1), NEG_INF))
    return jnp.stack(out, axis=1)


def _na_attention(qkv, kv_c, bias_mask, batch):
    t = qkv.shape[0] // batch
    tc = kv_c.shape[0] // batch
    qkv = qkv.reshape(batch, t, 3 * NA_WIDTH)
    kv_c = kv_c.reshape(batch, tc, 2 * NA_WIDTH)
    tq = NA_GROUPS_PER_STEP * NA_QROWS * GRID_W
    hb = NA_HEADS_PER_STEP
    dh = hb * NA_HEAD_DIM
    nhb = NA_HEADS // hb
    out = pl.pallas_call(
        _na_kernel,
        grid=(nhb, batch, t // tq),
        in_specs=[
            pl.BlockSpec((1, tq, dh), lambda h, b, g: (b, g, h)),
            pl.BlockSpec((1, t, dh), lambda h, b, g: (b, 0, nhb + h)),
            pl.BlockSpec((1, t, dh), lambda h, b, g: (b, 0, 2 * nhb + h)),
            pl.BlockSpec((1, tc, dh), lambda h, b, g: (b, 0, h)),
            pl.BlockSpec((1, tc, dh), lambda h, b, g: (b, 0, nhb + h)),
            pl.BlockSpec((hb,) + bias_mask.shape[1:], lambda h, b, g: (h, 0, 0, 0)),
        ],
        out_specs=pl.BlockSpec((1, tq, dh), lambda h, b, g: (b, g, h)),
        out_shape=jax.ShapeDtypeStruct((batch, t, NA_WIDTH), BF16),
        compiler_params=_compiler_params(("parallel", "parallel", "parallel"), 40 * 2**20),
        name="na_attention",
    )(qkv, qkv, qkv, kv_c, kv_c, bias_mask)
    return out.reshape(batch * t, NA_WIDTH)


def _rope_tables(t):
    pos = jnp.arange(t)
    inv = ROPE_THETA ** (-jnp.arange(ROPE_AX_PAIRS, dtype=F32) / ROPE_AX_PAIRS)
    ar = (pos // GRID_W).astype(F32)[:, None] * inv
    ac = (pos % GRID_W).astype(F32)[:, None] * inv
    ang = jnp.concatenate([ar, ar, ac, ac], axis=-1)
    cos, sin = jnp.cos(ang), jnp.sin(ang)
    first_half = (jnp.arange(MLA_ROPE) % (2 * ROPE_AX_PAIRS)) < ROPE_AX_PAIRS
    pad = ((0, 0), (0, ROPE_PAD - MLA_ROPE))
    cos_p = jnp.pad(cos, pad, constant_values=1.0)
    sin_a = jnp.pad(jnp.where(first_half, -sin, 0.0), pad)
    sin_b = jnp.pad(jnp.where(first_half, 0.0, sin), pad)
    return cos_p, sin_a, sin_b


def _identity_rope_tables(t):
    return jnp.ones((t, ROPE_PAD), F32), jnp.zeros((t, ROPE_PAD), F32), jnp.zeros((t, ROPE_PAD), F32)


def kernel(x, c, ctx, c_ctx, mod_w, mod_b, norm_mix, norm_ffn, mla_w_in, mla_q_norm, mla_w_uq, mla_kv_norm, mla_w_ukv, lru_conv_w, lru_conv_b, lru_gate_a_w, lru_gate_a_b, lru_gate_x_w, lru_gate_x_b, lru_lambda, mix_w_out, na_w_qkv, na_rel_bias, na_w_out, ffn_w_up, ffn_conv_w, ffn_conv_b, ffn_w_down, final_norm):
    batch, t, d = x.shape
    tc = ctx.shape[1]
    n_lat, n_ctx = batch * t, batch * tc

    cond_rows = 2 * SUBLANES
    cond = jnp.zeros((cond_rows, d), F32).at[:batch].set(c).at[batch].set(c_ctx)
    mod = _modulation(cond, mod_w, mod_b)

    def mod_parts(layer):
        lat = [mod[layer, :batch, k * d:(k + 1) * d].reshape(batch, 1, d) for k in range(6)]
        cx = [mod[layer, batch:batch + 1, k * d:(k + 1) * d].reshape(1, 1, d) for k in range(6)]
        return lat, cx

    x2 = x.reshape(n_lat, d)
    c2 = ctx.reshape(n_ctx, d)

    (sh1, sc1, g1, sh2, sc2, g2), (csh1, csc1, cg1, csh2, csc2, cg2) = mod_parts(0)
    w_in = mla_w_in[0]
    o_kr = MLA_Q_LORA + MLA_KV_LORA + MLA_ROPE
    w_in_p = jnp.concatenate(
        [w_in[:, :o_kr], jnp.zeros((d, ROPE_PAD - MLA_ROPE), w_in.dtype), w_in[:, o_kr:]], axis=1).astype(BF16)
    w_uq_p = jnp.pad(mla_w_uq[0].reshape(MLA_Q_LORA, MLA_HEADS, MLA_NOPE + MLA_ROPE),
                     ((0, 0), (0, 0), (0, ROPE_PAD - MLA_ROPE))).reshape(MLA_Q_LORA, MLA_HEADS * MLA_QK).astype(BF16)
    w_ukv = mla_w_ukv[0].astype(BF16)
    w_gates = jnp.concatenate([lru_gate_a_w[0, 0], lru_gate_x_w[0, 0], lru_gate_a_w[0, 1], lru_gate_x_w[0, 1]],
                              axis=-1).astype(BF16)
    w_mix = mix_w_out[0].astype(BF16)

    pm = functools.partial(_premix0, gain=norm_mix[0], w_in=w_in_p, q_norm=mla_q_norm[0], w_uq=w_uq_p,
                           kv_norm=mla_kv_norm[0], w_ukv=w_ukv)
    cos, sin_a, sin_b = _rope_tables(t)
    q_l, k_l, v_l, rx_l, ry_l = pm(x, shift=sh1, scale=sc1, cos=cos, sin_a=sin_a, sin_b=sin_b)
    cos, sin_a, sin_b = _identity_rope_tables(tc)
    q_c, k_c, v_c, rx_c, ry_c = pm(ctx, shift=csh1, scale=csc1, cos=cos, sin_a=sin_a, sin_b=sin_b)

    att_l = _mla_attention(q_l, [k_c, k_l], [v_c, v_l], tq=512, n_unit=2).reshape(n_lat, -1)
    att_c = _mla_attention(q_c, [k_c], [v_c], tq=tc, n_unit=1).reshape(n_ctx, -1)
    rec_l, rec_c = _rglru(rx_l, rx_c, ry_l, ry_c, batch, lru_conv_w[0], lru_conv_b[0], w_gates,
                          lru_gate_a_b[0], lru_gate_x_b[0], lru_lambda[0])

    x2 = _mixout(att_l, rec_l, 0, 0, w_mix, x2, g1, t)
    c2 = _mixout(att_c, rec_c, 0, 0, w_mix, c2, cg1, tc)

    ffn0 = functools.partial(_ffn, gain=norm_ffn[0], w_up=ffn_w_up[0].astype(BF16), conv_w=ffn_conv_w[0],
                             conv_b=ffn_conv_b[0], w_down=ffn_w_down[0].astype(BF16), final_gain=None)
    x2 = ffn0(x2, shift=sh2, scale=sc2, gate=g2, seq_len=t, rows_per_mod=t)
    c2 = ffn0(c2, shift=csh2, scale=csc2, gate=cg2, seq_len=tc, rows_per_mod=tc)

    (sh1, sc1, g1, sh2, sc2, g2), (csh1, csc1, _, _, _, _) = mod_parts(1)
    w_qkv = na_w_qkv[0].astype(BF16)
    qkv = _premix1(x2, norm_mix[1], sh1, sc1, w_qkv, t, NA_WIDTH)
    kv_c = _premix1(c2, norm_mix[1], csh1, csc1, w_qkv, tc, 0, col0=NA_WIDTH)
    o = _na_attention(qkv, kv_c, _na_bias_mask(na_rel_bias[0], t // GRID_W), batch)
    x2 = _mixout(o, o, 0, 1, na_w_out[0].astype(BF16), x2, g1, t)
    x2 = _ffn(x2, norm_ffn[1], sh2, sc2, g2, ffn_w_up[1].astype(BF16), ffn_conv_w[1], ffn_conv_b[1],
              ffn_w_down[1].astype(BF16), final_norm, seq_len=t, rows_per_mod=t)
    return x2.reshape(batch, t, d)
```

```python
import functools

import numpy as np
import jax
import jax.numpy as jnp
from jax import lax
from jax.experimental import pallas as pl
from jax.experimental.pallas import tpu as pltpu

D_MODEL = 2048
DEPTH = 2
GRID_W = 64
NORM_EPS = 1e-6
NEG_INF = -1e30

MLA_HEADS = D_MODEL // 256
MLA_Q_LORA = D_MODEL // 4
MLA_KV_LORA = D_MODEL // 8
MLA_NOPE = 128
MLA_ROPE = 64
MLA_V = 128
MLA_SCALE = (MLA_NOPE + MLA_ROPE) ** -0.5
ROPE_THETA = 10000.0
ROPE_AX_PAIRS = MLA_ROPE // 4

LRU_WIDTH = D_MODEL // 2
LRU_BLOCKS = 8
LRU_BLOCK_W = LRU_WIDTH // LRU_BLOCKS
LRU_CONV = 4
LRU_C = 8.0

NA_HEADS = 16
NA_HEAD_DIM = D_MODEL // NA_HEADS
NA_WIDTH = NA_HEADS * NA_HEAD_DIM
NA_SCALE = NA_HEAD_DIM ** -0.5
NA_ROWS = 8
NA_COLS = 16

FFN_HIDDEN = (D_MODEL * 11) // 4

LANES = 128
SUBLANES = 8
VMEM_BYTES = 64 * 1024 * 1024
ROPE_PAD = LANES
MLA_QK = MLA_NOPE + ROPE_PAD

BF16 = jnp.bfloat16
F32 = jnp.float32


def _compiler_params(semantics, vmem_bytes):
    limit = min(int(vmem_bytes), VMEM_BYTES - 4 * 1024 * 1024)
    return pltpu.CompilerParams(dimension_semantics=semantics, vmem_limit_bytes=limit)


def _resident(shape, index_map):
    return pl.BlockSpec(shape, index_map, pipeline_mode=pl.Buffered(1))


def _rms(x, g):
    return x * lax.rsqrt(jnp.mean(x * x, axis=-1, keepdims=True) + NORM_EPS) * g


def _norm_mod(x, gain, shift, scale):
    return _rms(x, gain) * (1.0 + scale) + shift


def _gelu(x):
    return 0.5 * x * (1.0 + jnp.tanh(0.7978845608028654 * (x + 0.044715 * (x * x * x))))


def _dot(a, b):
    return jnp.dot(a, b, preferred_element_type=F32)


def _dot_nt(a, b):
    return lax.dot_general(a, b, (((1,), (1,)), ((), ())), preferred_element_type=F32)


def _dot_tn(a, b):
    return lax.dot_general(a, b, (((0,), (0,)), ((), ())), preferred_element_type=F32)


def _mod_kernel(c_ref, w_ref, b_ref, o_ref):
    c = c_ref[...]
    s = (c * jax.nn.sigmoid(c)).astype(BF16)
    o_ref[0] = _dot(s, w_ref[0].astype(BF16)) + b_ref[0]


def _modulation(cond, mod_w, mod_b, tn=1024):
    rows = cond.shape[0]
    depth, d, n = mod_w.shape
    return pl.pallas_call(
        _mod_kernel,
        grid=(depth, n // tn),
        in_specs=[
            pl.BlockSpec((rows, d), lambda l, j: (0, 0)),
            pl.BlockSpec((1, d, tn), lambda l, j: (l, 0, j)),
            pl.BlockSpec((1, 1, tn), lambda l, j: (l, 0, j)),
        ],
        out_specs=pl.BlockSpec((1, rows, tn), lambda l, j: (l, 0, j)),
        out_shape=jax.ShapeDtypeStruct((depth, rows, n), F32),
        compiler_params=_compiler_params(("parallel", "parallel"), 40 * 2**20),
        name="modulation",
    )(cond, mod_w, mod_b.reshape(depth, 1, n))


def _rope(x, cos, sin_a, sin_b):
    return x * cos + pltpu.roll(x, LANES - ROPE_AX_PAIRS, 1) * sin_a + pltpu.roll(x, ROPE_AX_PAIRS, 1) * sin_b


def _premix0_kernel(x_ref, gain_ref, sh_ref, sc_ref, win_ref, qn_ref, wuq_ref, kvn_ref, wukv_ref,
                    cos_ref, sina_ref, sinb_ref, q_ref, k_ref, v_ref, rx_ref, ry_ref):
    h = _norm_mod(x_ref[...], gain_ref[...], sh_ref[0], sc_ref[0]).astype(BF16)
    p = _dot(h, win_ref[...])
    o_kv = MLA_Q_LORA
    o_kr = o_kv + MLA_KV_LORA
    o_rx = o_kr + ROPE_PAD
    o_ry = o_rx + LRU_WIDTH
    rx_ref[...] = p[:, o_rx:o_ry]
    ry_ref[...] = p[:, o_ry:]
    cos, sin_a, sin_b = cos_ref[...], sina_ref[...], sinb_ref[...]
    qn = _rms(p[:, :o_kv], qn_ref[...]).astype(BF16)
    q = _dot(qn, wuq_ref[...]) * MLA_SCALE
    cn = _rms(p[:, o_kv:o_kr], kvn_ref[...]).astype(BF16)
    kv = _dot(cn, wukv_ref[...])
    kr = _rope(p[:, o_kr:o_rx], cos, sin_a, sin_b).astype(BF16)
    for hd in range(MLA_HEADS):
        qo = hd * MLA_QK
        q_ref[0, hd, :, :MLA_NOPE] = q[:, qo:qo + MLA_NOPE].astype(BF16)
        q_ref[0, hd, :, MLA_NOPE:] = _rope(q[:, qo + MLA_NOPE:qo + MLA_QK], cos, sin_a, sin_b).astype(BF16)
        ko = hd * (MLA_NOPE + MLA_V)
        k_ref[0, hd, :, :MLA_NOPE] = kv[:, ko:ko + MLA_NOPE].astype(BF16)
        k_ref[0, hd, :, MLA_NOPE:] = kr
        v_ref[0, hd] = kv[:, ko + MLA_NOPE:ko + MLA_NOPE + MLA_V].astype(BF16)


def _premix0(x, gain, shift, scale, w_in, q_norm, w_uq, kv_norm, w_ukv, cos, sin_a, sin_b, tm=256):
    b, t, d = x.shape
    n = b * t
    tpb = t // tm
    nmod = shift.shape[0]
    mod_idx = (lambda i: (i // tpb, 0, 0)) if nmod > 1 else (lambda i: (0, 0, 0))
    const2 = lambda i: (0, 0)
    in_cols = w_in.shape[1]
    head_spec = lambda w: pl.BlockSpec((1, MLA_HEADS, tm, w), lambda i: (i // tpb, 0, i % tpb, 0))
    return pl.pallas_call(
        _premix0_kernel,
        grid=(n // tm,),
        in_specs=[
            pl.BlockSpec((tm, d), lambda i: (i, 0)),
            _resident((1, d), const2),
            pl.BlockSpec((1, 1, d), mod_idx),
            pl.BlockSpec((1, 1, d), mod_idx),
            _resident((d, in_cols), const2),
            _resident((1, MLA_Q_LORA), const2),
            _resident(w_uq.shape, const2),
            _resident((1, MLA_KV_LORA), const2),
            _resident(w_ukv.shape, const2),
            pl.BlockSpec((tm, ROPE_PAD), lambda i: (i % tpb, 0)),
            pl.BlockSpec((tm, ROPE_PAD), lambda i: (i % tpb, 0)),
            pl.BlockSpec((tm, ROPE_PAD), lambda i: (i % tpb, 0)),
        ],
        out_specs=[
            head_spec(MLA_QK), head_spec(MLA_QK), head_spec(MLA_V),
            pl.BlockSpec((tm, LRU_WIDTH), lambda i: (i, 0)),
            pl.BlockSpec((tm, LRU_WIDTH), lambda i: (i, 0)),
        ],
        out_shape=[
            jax.ShapeDtypeStruct((b, MLA_HEADS, t, MLA_QK), BF16),
            jax.ShapeDtypeStruct((b, MLA_HEADS, t, MLA_QK), BF16),
            jax.ShapeDtypeStruct((b, MLA_HEADS, t, MLA_V), BF16),
            jax.ShapeDtypeStruct((n, LRU_WIDTH), F32),
            jax.ShapeDtypeStruct((n, LRU_WIDTH), F32),
        ],
        compiler_params=_compiler_params(("parallel",), 48 * 2**20),
        name="premix0",
    )(x.reshape(n, d), gain.reshape(1, d), shift, scale, w_in, q_norm.reshape(1, -1), w_uq,
      kv_norm.reshape(1, -1), w_ukv, cos, sin_a, sin_b)


def _softmax_pv_t(scores_t, values):
    m = functools.reduce(jnp.maximum, [jnp.max(x, axis=0, keepdims=True) for x in scores_t])
    p = [jnp.exp(x - m) for x in scores_t]
    l = functools.reduce(jnp.add, [jnp.sum(x, axis=0, keepdims=True) for x in p])
    o_t = functools.reduce(jnp.add, [_dot_tn(v, x.astype(BF16)) for x, v in zip(p, values)])
    return (o_t / l).T


def _attn_kernel(*refs, n_seg, tq):
    q_ref = refs[0]
    k_refs = refs[1:1 + n_seg]
    v_refs = refs[1 + n_seg:1 + 2 * n_seg]
    o_ref = refs[1 + 2 * n_seg]
    for r0 in range(0, q_ref.shape[2], tq):
        q = q_ref[0, 0, r0:r0 + tq, :]
        s_t = [_dot_nt(k[0, 0], q) for k in k_refs]
        o_ref[0, r0:r0 + tq, :] = _softmax_pv_t(s_t, [v[0, 0] for v in v_refs]).astype(o_ref.dtype)


def _mla_attention(q, ks, vs, tq, n_unit):
    b, h, t, dk = q.shape
    dv = vs[0].shape[-1]
    n_seg = len(ks)
    tb = tq * n_unit
    kv_spec = lambda a: pl.BlockSpec((1, 1) + a.shape[2:], lambda bi, hi, qi: (bi, hi, 0, 0))
    return pl.pallas_call(
        functools.partial(_attn_kernel, n_seg=n_seg, tq=tq),
        grid=(b, h, t // tb),
        in_specs=[pl.BlockSpec((1, 1, tb, dk), lambda bi, hi, qi: (bi, hi, qi, 0))]
        + [kv_spec(a) for a in ks] + [kv_spec(a) for a in vs],
        out_specs=pl.BlockSpec((1, tb, dv), lambda bi, hi, qi: (bi, qi, hi)),
        out_shape=jax.ShapeDtypeStruct((b, t, h * dv), BF16),
        compiler_params=_compiler_params(("parallel", "parallel", "parallel"), 40 * 2**20),
        name="mla_attention",
    )(q, *ks, *vs)


def _scan_tile(a, b, row, reverse):
    for d in (1, 2, 4):
        if reverse:
            keep = row < SUBLANES - d
            shift = SUBLANES - d
        else:
            keep = row >= d
            shift = d
        a_s = jnp.where(keep, pltpu.roll(a, shift, 0), 1.0)
        b_s = jnp.where(keep, pltpu.roll(b, shift, 0), 0.0)
        b = a * b_s + b
        a = a * a_s
    return a, b


def _rglru_kernel(rxl_ref, rxc_ref, ryl_ref, ryc_ref, cw_ref, cb_ref, wg_ref, gab_ref, gxb_ref, lam_ref,
                  ol_ref, oc_ref, xpad, af, bf, ab, bb, *, chunk):
    t = rxl_ref.shape[0]
    tc = rxc_ref.shape[0]
    width = rxl_ref.shape[1]
    cw = cw_ref[...]
    cb = cb_ref[...]
    z = -lam_ref[...]
    sp = jnp.maximum(z, 0.0) + jnp.log1p(jnp.exp(-jnp.abs(z)))
    gab = gab_ref[...]
    gxb = gxb_ref[...]
    pad = SUBLANES

    def coeffs(x_ref, n, off_f, off_b):
        xpad[0:pad, :] = jnp.zeros((pad, width), F32)
        xpad[pad:pad + n, :] = x_ref[...]
        xpad[pad + n:2 * pad + n, :] = jnp.zeros((pad, width), F32)
        for c0 in range(0, n, chunk):
            xc = cb
            for tap in range(LRU_CONV):
                xc = xc + cw[tap:tap + 1] * xpad[pad - 2 + tap + c0:pad - 2 + tap + c0 + chunk, :]
            xb = xc.astype(BF16)
            for blk in range(width // LRU_BLOCK_W):
                sl = slice(blk * LRU_BLOCK_W, (blk + 1) * LRU_BLOCK_W)
                g = _dot(xb[:, sl], wg_ref[blk])
                xs = xc[:, sl]
                for d, (a_ref, b_ref, off) in enumerate(((af, bf, off_f), (ab, bb, off_b))):
                    g0 = 2 * d * LRU_BLOCK_W
                    r = jax.nn.sigmoid(g[:, g0:g0 + LRU_BLOCK_W] + gab[d:d + 1, sl])
                    i = jax.nn.sigmoid(g[:, g0 + LRU_BLOCK_W:g0 + 2 * LRU_BLOCK_W] + gxb[d:d + 1, sl])
                    log_a = -LRU_C * r * sp[d:d + 1, sl]
                    a = jnp.exp(log_a)
                    one_minus_a2 = -jnp.tanh(log_a) * (a * a + 1.0)
                    a_ref[off + c0:off + c0 + chunk, sl] = a
                    b_ref[off + c0:off + c0 + chunk, sl] = jnp.sqrt(one_minus_a2) * (i * xs)

    coeffs(rxc_ref, tc, 0, t)
    coeffs(rxl_ref, t, tc, 0)

    n_tiles = (t + tc) // SUBLANES
    row = lax.broadcasted_iota(jnp.int32, (SUBLANES, width), 0)

    def body(k, carry):
        hf, hb = carry
        r0 = pl.multiple_of(k * SUBLANES, SUBLANES)
        a, b = _scan_tile(af[pl.ds(r0, SUBLANES), :], bf[pl.ds(r0, SUBLANES), :], row, False)
        h = a * hf + b
        bf[pl.ds(r0, SUBLANES), :] = h
        hf = h[SUBLANES - 1:SUBLANES]
        r1 = pl.multiple_of((n_tiles - 1 - k) * SUBLANES, SUBLANES)
        a, b = _scan_tile(ab[pl.ds(r1, SUBLANES), :], bb[pl.ds(r1, SUBLANES), :], row, True)
        h = a * hb + b
        bb[pl.ds(r1, SUBLANES), :] = h
        hb = h[0:1]
        return hf, hb

    zero = jnp.zeros((1, width), F32)
    lax.fori_loop(0, n_tiles, body, (zero, zero), unroll=4)

    for c0 in range(0, t, chunk):
        rec = bf[tc + c0:tc + c0 + chunk, :] + bb[c0:c0 + chunk, :]
        ol_ref[c0:c0 + chunk, :] = (rec * _gelu(ryl_ref[c0:c0 + chunk, :])).astype(ol_ref.dtype)
    for c0 in range(0, tc, chunk):
        rec = bf[c0:c0 + chunk, :] + bb[t + c0:t + c0 + chunk, :]
        oc_ref[c0:c0 + chunk, :] = (rec * _gelu(ryc_ref[c0:c0 + chunk, :])).astype(oc_ref.dtype)


def _rglru(rx_l, rx_c, ry_l, ry_c, batch, conv_w, conv_b, w_gates, ga_b, gx_b, lam, width=256, chunk=256):
    t = rx_l.shape[0] // batch
    tc = rx_c.shape[0] // batch
    nblk = width // LRU_BLOCK_W
    lat = pl.BlockSpec((t, width), lambda bi, ci: (bi, ci))
    ctx = pl.BlockSpec((tc, width), lambda bi, ci: (bi, ci))
    par = lambda rows: pl.BlockSpec((rows, width), lambda bi, ci: (0, ci))
    seq = t + tc
    return pl.pallas_call(
        functools.partial(_rglru_kernel, chunk=chunk),
        grid=(batch, LRU_WIDTH // width),
        in_specs=[lat, ctx, lat, ctx, par(LRU_CONV), par(1),
                  pl.BlockSpec((nblk,) + w_gates.shape[1:], lambda bi, ci: (ci, 0, 0)),
                  par(2), par(2), par(2)],
        out_specs=[lat, ctx],
        out_shape=[jax.ShapeDtypeStruct(rx_l.shape, BF16), jax.ShapeDtypeStruct(rx_c.shape, BF16)],
        scratch_shapes=[pltpu.VMEM((t + 2 * SUBLANES, width), F32)] + [pltpu.VMEM((seq, width), F32)] * 4,
        compiler_params=_compiler_params(("parallel", "parallel"), 40 * 2**20),
        name="rglru",
    )(rx_l, rx_c, ry_l, ry_c, conv_w, conv_b.reshape(1, -1), w_gates, ga_b, gx_b, lam)


def _mixout_kernel(a_ref, b_ref, wa_ref, wb_ref, x_ref, g_ref, gain_ref, sh_ref, sc_ref, o_ref, h_ref):
    tm = x_ref.shape[0]
    for r0 in range(0, tm, tm // 2):
        rs = slice(r0, r0 + tm // 2)
        y = x_ref[rs, :] + g_ref[0] * (_dot(a_ref[rs, :], wa_ref[...]) + _dot(b_ref[rs, :], wb_ref[...]))
        o_ref[rs, :] = y
        h_ref[rs, :] = _norm_mod(y, gain_ref[...], sh_ref[0], sc_ref[0]).astype(h_ref.dtype)


def _mixout(a, b, a_blk, b_blk, w, x, gate, ffn_gain, ffn_shift, ffn_scale, rows_per_mod, tm=512):
    n, d = x.shape
    kh = w.shape[0] // 2
    nmod = gate.shape[0]
    mod_idx = (lambda i: ((i * tm) // rows_per_mod, 0, 0)) if nmod > 1 else (lambda i: (0, 0, 0))
    row_spec = pl.BlockSpec((tm, d), lambda i: (i, 0))
    mod_spec = pl.BlockSpec((1, 1, d), mod_idx)
    return pl.pallas_call(
        _mixout_kernel,
        grid=(n // tm,),
        in_specs=[
            pl.BlockSpec((tm, kh), lambda i: (i, a_blk)),
            pl.BlockSpec((tm, kh), lambda i: (i, b_blk)),
            _resident((kh, d), lambda i: (0, 0)),
            _resident((kh, d), lambda i: (1, 0)),
            row_spec, mod_spec,
            _resident((1, d), lambda i: (0, 0)),
            mod_spec, mod_spec,
        ],
        out_specs=[row_spec, row_spec],
        out_shape=[jax.ShapeDtypeStruct((n, d), F32), jax.ShapeDtypeStruct((n, d), BF16)],
        compiler_params=_compiler_params(("parallel",), 48 * 2**20),
        name="mixout",
    )(a, b, w, w, x, gate, ffn_gain.reshape(1, d), ffn_shift, ffn_scale)


def _ffn_kernel(h_ref, hp_ref, hn_ref, xc_ref, gt_ref, wg_ref, wu_ref, cw_ref, cb_ref,
                wd_ref, fn_ref, o_ref, gpad, act_scr, *y_scr, seq_len, nj, nd):
    i = pl.program_id(0)
    j = pl.program_id(1)
    tm = h_ref.shape[0]
    th = act_scr.shape[2]
    td = wd_ref.shape[1]
    halo = SUBLANES

    @pl.when(j < nj)
    def _():
        h = h_ref[...]
        edge = jnp.concatenate([hp_ref[...].astype(F32)[halo:], hn_ref[...].astype(F32)[:halo]], axis=0)
        g_ext = _dot(jnp.concatenate([h, edge.astype(BF16)], axis=0), wg_ref[...])
        g = g_ext[:tm]
        gpad[halo - 1:halo, :] = g_ext[tm + halo - 1:tm + halo]
        gpad[halo:halo + tm, :] = g
        gpad[halo + tm:halo + tm + 1, :] = g_ext[tm + halo:tm + halo + 1]
        pos = (i * tm + lax.broadcasted_iota(jnp.int32, (tm, 1), 0)) & (seq_len - 1)
        g_prev = jnp.where(pos == 0, 0.0, gpad[halo - 1:halo - 1 + tm, :])
        g_next = jnp.where(pos == seq_len - 1, 0.0, gpad[halo + 1:halo + 1 + tm, :])
        cw = cw_ref[...]
        conv = cb_ref[...] + cw[0:1] * g_prev + cw[1:2] * g + cw[2:3] * g_next
        act_scr[j] = (_gelu(conv) * _dot(h, wu_ref[...])).astype(BF16)

    @pl.when(j >= nj)
    def _():
        gate = gt_ref[0]
        for r0 in range(0, tm, tm // 2):
            rs = slice(r0, r0 + tm // 2)
            acc = _dot(act_scr[0, rs, :], wd_ref[0:th, :])
            for k in range(1, nj):
                acc = acc + _dot(act_scr[k, rs, :], wd_ref[k * th:(k + 1) * th, :])
            y = xc_ref[rs, :] + gate * acc
            if y_scr:
                y_scr[0][j - nj, rs, :] = y
            else:
                o_ref[rs, :] = y

    if y_scr:
        @pl.when(j == nj + nd - 1)
        def _():
            ys = y_scr[0]
            ssq = functools.reduce(jnp.add, [jnp.sum(ys[k] * ys[k], axis=-1, keepdims=True) for k in range(nd)])
            inv = lax.rsqrt(ssq / (nd * td) + NORM_EPS)
            for k in range(nd):
                o_ref[:, k * td:(k + 1) * td] = ys[k] * inv * fn_ref[:, k * td:(k + 1) * td]


def _ffn(x, h, gate, w_up, conv_w, conv_b, w_down, final_gain, seq_len, rows_per_mod, tm=512, th=512, td=512):
    n, d = x.shape
    hidden = w_down.shape[0]
    nj = hidden // th
    nd = d // td
    final_norm = final_gain is not None
    assert seq_len & (seq_len - 1) == 0 and (tm % seq_len == 0 or seq_len % tm == 0)
    nmod = gate.shape[0]
    mod_row = (lambda i: (i * tm) // rows_per_mod) if nmod > 1 else (lambda i: 0)
    edge_rows = 2 * SUBLANES
    bpt = tm // edge_rows
    last_blk = n // edge_rows - 1
    up_col = lambda j: jnp.minimum(j, nj - 1)
    down_col = lambda j: jnp.maximum(j - nj, 0)
    wd_col = lambda j: jnp.where(j == 0, nd - 1, down_col(j))
    if final_norm:
        out_spec = pl.BlockSpec((tm, d), lambda i, j: (i, 0))
        y_scratch = [pltpu.VMEM((nd, tm, td), F32)]
    else:
        out_spec = pl.BlockSpec((tm, td), lambda i, j: (i, down_col(j)))
        y_scratch = []
        final_gain = jnp.zeros((d,), F32)
    return pl.pallas_call(
        functools.partial(_ffn_kernel, seq_len=seq_len, nj=nj, nd=nd),
        grid=(n // tm, nj + nd),
        in_specs=[
            pl.BlockSpec((tm, d), lambda i, j: (i, 0)),
            pl.BlockSpec((edge_rows, d), lambda i, j: (jnp.maximum(i * bpt - 1, 0), 0)),
            pl.BlockSpec((edge_rows, d), lambda i, j: (jnp.minimum((i + 1) * bpt, last_blk), 0)),
            pl.BlockSpec((tm, td), lambda i, j: (i, down_col(j))),
            pl.BlockSpec((1, 1, td), lambda i, j: (mod_row(i), 0, down_col(j))),
            pl.BlockSpec((d, th), lambda i, j: (0, up_col(j))),
            pl.BlockSpec((d, th), lambda i, j: (0, up_col(j) + nj)),
            pl.BlockSpec((conv_w.shape[0], th), lambda i, j: (0, up_col(j))),
            pl.BlockSpec((1, th), lambda i, j: (0, up_col(j))),
            pl.BlockSpec((hidden, td), lambda i, j: (0, wd_col(j))),
            _resident((1, d), lambda i, j: (0, 0)),
        ],
        out_specs=out_spec,
        out_shape=jax.ShapeDtypeStruct((n, d), F32),
        scratch_shapes=[
            pltpu.VMEM((tm + 2 * SUBLANES, th), F32),
            pltpu.VMEM((nj, tm, th), BF16),
        ] + y_scratch,
        compiler_params=_compiler_params(("parallel", "arbitrary"), 56 * 2**20),
        name="conv_ffn",
    )(h, h, h, x, gate, w_up, w_up, conv_w, conv_b.reshape(1, -1), w_down, final_gain.reshape(1, d))


def _premix1_kernel(x_ref, gain_ref, sh_ref, sc_ref, w_ref, o_ref, h_scr, *, n_q_tiles):
    j = pl.program_id(1)

    @pl.when(j == 0)
    def _():
        h_scr[...] = _norm_mod(x_ref[...], gain_ref[...], sh_ref[0], sc_ref[0]).astype(BF16)

    y = _dot(h_scr[...], w_ref[...])
    if n_q_tiles:
        y = y * jnp.where(j < n_q_tiles, NA_SCALE, 1.0)
    o_ref[...] = y.astype(o_ref.dtype)


def _premix1(x, gain, shift, scale, w, rows_per_mod, n_q_cols, col0=0, tm=1024, tn=1024):
    n, d = x.shape
    cols = w.shape[1] - col0
    j0 = col0 // tn
    nmod = shift.shape[0]
    mod_idx = (lambda i, j: ((i * tm) // rows_per_mod, 0, 0)) if nmod > 1 else (lambda i, j: (0, 0, 0))
    return pl.pallas_call(
        functools.partial(_premix1_kernel, n_q_tiles=n_q_cols // tn),
        grid=(n // tm, cols // tn),
        in_specs=[
            pl.BlockSpec((tm, d), lambda i, j: (i, 0)),
            _resident((1, d), lambda i, j: (0, 0)),
            pl.BlockSpec((1, 1, d), mod_idx),
            pl.BlockSpec((1, 1, d), mod_idx),
            pl.BlockSpec((d, tn), lambda i, j: (0, j + j0)),
        ],
        out_specs=pl.BlockSpec((tm, tn), lambda i, j: (i, j)),
        out_shape=jax.ShapeDtypeStruct((n, cols), BF16),
        scratch_shapes=[pltpu.VMEM((tm, d), BF16)],
        compiler_params=_compiler_params(("parallel", "arbitrary"), 48 * 2**20),
        name="premix1",
    )(x, gain.reshape(1, d), shift, scale, w)


NA_QROWS = 8
NA_BAND = NA_QROWS + NA_ROWS
NA_GROUPS_PER_STEP = 2


def _na_patterns(rows):
    n_groups = rows // NA_QROWS
    assert (n_groups - 2) * NA_QROWS - NA_ROWS // 2 <= rows - NA_BAND
    a = np.arange(NA_QROWS)
    offs, los = [], []
    for g in (0, 1, n_groups - 1):
        r0 = g * NA_QROWS
        u0 = int(np.clip(r0 - NA_ROWS // 2, 0, rows - NA_BAND))
        los.append([int(v) for v in np.clip(r0 + a - NA_ROWS // 2, 0, rows - NA_ROWS) - u0])
        offs.append(u0 - r0 + NA_ROWS - 1)
    pad_lo = max(0, NA_QROWS - 1 - min(offs))
    pad_hi = max(0, max(offs) + NA_BAND - (2 * NA_ROWS - 1))
    return offs, los, pad_lo, pad_hi


def _na_kernel(q_ref, k_ref, v_ref, kc_ref, vc_ref, tbl_ref, o_ref, bm_scr):
    rows = k_ref.shape[1] // GRID_W
    n_groups = rows // NA_QROWS
    tq = NA_QROWS * GRID_W
    tk = NA_BAND * GRID_W

    @pl.when((pl.program_id(1) == 0) & (pl.program_id(2) == 0))
    def _():
        offs, los, pad_lo, _ = _na_patterns(rows)
        c_idx = lax.broadcasted_iota(jnp.int32, (tk, LANES), 0) // GRID_W
        second = lax.broadcasted_iota(jnp.int32, (tk, LANES), 1) >= GRID_W
        for p, (off, lo) in enumerate(zip(offs, los)):
            for ap in range(NA_QROWS // 2):
                start = off - 2 * ap + pad_lo - 1
                blk = tbl_ref[0, start:start + NA_BAND].reshape(tk, LANES)
                lo_l = jnp.where(second, lo[2 * ap + 1], lo[2 * ap])
                ok = (c_idx >= lo_l) & (c_idx < lo_l + NA_ROWS)
                bm_scr[p, :, ap * LANES:(ap + 1) * LANES] = jnp.where(ok, blk, NEG_INF)

    for gg in range(NA_GROUPS_PER_STEP):
        g = pl.program_id(2) * NA_GROUPS_PER_STEP + gg
        u0 = jnp.clip(g * NA_QROWS - NA_ROWS // 2, 0, rows - NA_BAND)
        start = pl.multiple_of(u0 * GRID_W, GRID_W)
        pat = jnp.where(g == 0, 0, jnp.where(g == n_groups - 1, 2, 1))
        qs = slice(gg * tq, (gg + 1) * tq)
        q = q_ref[0, qs, :]
        k_band = k_ref[0, pl.ds(start, tk), :]
        v_band = v_ref[0, pl.ds(start, tk), :]
        s_b = _dot_nt(k_band, q) + bm_scr[pat]
        s_c = _dot_nt(kc_ref[0], q)
        o_ref[0, qs, :] = _softmax_pv_t([s_b, s_c], [v_band, vc_ref[0]]).astype(o_ref.dtype)


def _na_bias_table(rel_bias, rows):
    heads, n_dr, n_dc = rel_bias.shape
    qc = np.arange(GRID_W)[:, None]
    kc = np.arange(GRID_W)[None, :]
    cs = np.clip(qc - NA_COLS // 2, 0, GRID_W - NA_COLS)
    col_ok = (kc >= cs) & (kc < cs + NA_COLS)
    lead = GRID_W - NA_COLS
    ext = jnp.pad(rel_bias.astype(F32), ((0, 0), (0, 0), (lead, lead)))
    width = 2 * GRID_W - 1
    skew = jnp.broadcast_to(ext[:, :, None, :], (heads, n_dr, GRID_W, width))
    skew = jnp.pad(skew, ((0, 0), (0, 0), (0, 0), (0, 1))).reshape(heads, n_dr, GRID_W * (width + 1))
    skew = skew[:, :, :GRID_W * width].reshape(heads, n_dr, GRID_W, width)
    toep = skew[..., GRID_W - 1:]
    toep = jnp.where(col_ok, toep, NEG_INF).transpose(0, 1, 3, 2)
    _, _, pad_lo, pad_hi = _na_patterns(rows)
    toep = jnp.pad(toep, ((0, 0), (pad_lo, pad_hi), (0, 0), (0, 0)))
    return jnp.concatenate([toep[:, 1:], toep[:, :-1]], axis=-1)


def _na_attention(qkv, kv_c, bias_table, batch):
    t = qkv.shape[0] // batch
    tc = kv_c.shape[0] // batch
    qkv = qkv.reshape(batch, t, 3 * NA_WIDTH)
    kv_c = kv_c.reshape(batch, tc, 2 * NA_WIDTH)
    tq = NA_GROUPS_PER_STEP * NA_QROWS * GRID_W
    dh = NA_HEAD_DIM
    out = pl.pallas_call(
        _na_kernel,
        grid=(NA_HEADS, batch, t // tq),
        in_specs=[
            pl.BlockSpec((1, tq, dh), lambda h, b, g: (b, g, h)),
            pl.BlockSpec((1, t, dh), lambda h, b, g: (b, 0, NA_HEADS + h)),
            pl.BlockSpec((1, t, dh), lambda h, b, g: (b, 0, 2 * NA_HEADS + h)),
            pl.BlockSpec((1, tc, dh), lambda h, b, g: (b, 0, h)),
            pl.BlockSpec((1, tc, dh), lambda h, b, g: (b, 0, NA_HEADS + h)),
            pl.BlockSpec((1,) + bias_table.shape[1:], lambda h, b, g: (h, 0, 0, 0)),
        ],
        out_specs=pl.BlockSpec((1, tq, dh), lambda h, b, g: (b, g, h)),
        out_shape=jax.ShapeDtypeStruct((batch, t, NA_WIDTH), BF16),
        scratch_shapes=[pltpu.VMEM((3, NA_BAND * GRID_W, NA_QROWS * GRID_W), F32)],
        compiler_params=_compiler_params(("parallel", "arbitrary", "arbitrary"), 40 * 2**20),
        name="na_attention",
    )(qkv, qkv, qkv, kv_c, kv_c, bias_table)
    return out.reshape(batch * t, NA_WIDTH)


def _rope_tables(t):
    pos = jnp.arange(t)
    inv = ROPE_THETA ** (-jnp.arange(ROPE_AX_PAIRS, dtype=F32) / ROPE_AX_PAIRS)
    ar = (pos // GRID_W).astype(F32)[:, None] * inv
    ac = (pos % GRID_W).astype(F32)[:, None] * inv
    ang = jnp.concatenate([ar, ar, ac, ac], axis=-1)
    cos, sin = jnp.cos(ang), jnp.sin(ang)
    first_half = (jnp.arange(MLA_ROPE) % (2 * ROPE_AX_PAIRS)) < ROPE_AX_PAIRS
    pad = ((0, 0), (0, ROPE_PAD - MLA_ROPE))
    cos_p = jnp.pad(cos, pad, constant_values=1.0)
    sin_a = jnp.pad(jnp.where(first_half, -sin, 0.0), pad)
    sin_b = jnp.pad(jnp.where(first_half, 0.0, sin), pad)
    return cos_p, sin_a, sin_b


def _identity_rope_tables(t):
    return jnp.ones((t, ROPE_PAD), F32), jnp.zeros((t, ROPE_PAD), F32), jnp.zeros((t, ROPE_PAD), F32)


def kernel(x, c, ctx, c_ctx, mod_w, mod_b, norm_mix, norm_ffn, mla_w_in, mla_q_norm, mla_w_uq, mla_kv_norm, mla_w_ukv, lru_conv_w, lru_conv_b, lru_gate_a_w, lru_gate_a_b, lru_gate_x_w, lru_gate_x_b, lru_lambda, mix_w_out, na_w_qkv, na_rel_bias, na_w_out, ffn_w_up, ffn_conv_w, ffn_conv_b, ffn_w_down, final_norm):
    batch, t, d = x.shape
    tc = ctx.shape[1]
    n_lat, n_ctx = batch * t, batch * tc

    cond_rows = 2 * SUBLANES
    cond = jnp.zeros((cond_rows, d), F32).at[:batch].set(c).at[batch].set(c_ctx)
    mod = _modulation(cond, mod_w, mod_b)

    def mod_parts(layer):
        lat = [mod[layer, :batch, k * d:(k + 1) * d].reshape(batch, 1, d) for k in range(6)]
        cx = [mod[layer, batch:batch + 1, k * d:(k + 1) * d].reshape(1, 1, d) for k in range(6)]
        return lat, cx

    x2 = x.reshape(n_lat, d)
    c2 = ctx.reshape(n_ctx, d)

    (sh1, sc1, g1, sh2, sc2, g2), (csh1, csc1, cg1, csh2, csc2, cg2) = mod_parts(0)
    w_in = mla_w_in[0]
    o_kr = MLA_Q_LORA + MLA_KV_LORA + MLA_ROPE
    w_in_p = jnp.concatenate(
        [w_in[:, :o_kr], jnp.zeros((d, ROPE_PAD - MLA_ROPE), w_in.dtype), w_in[:, o_kr:]], axis=1).astype(BF16)
    w_uq_p = jnp.pad(mla_w_uq[0].reshape(MLA_Q_LORA, MLA_HEADS, MLA_NOPE + MLA_ROPE),
                     ((0, 0), (0, 0), (0, ROPE_PAD - MLA_ROPE))).reshape(MLA_Q_LORA, MLA_HEADS * MLA_QK).astype(BF16)
    w_ukv = mla_w_ukv[0].astype(BF16)
    w_gates = jnp.concatenate([lru_gate_a_w[0, 0], lru_gate_x_w[0, 0], lru_gate_a_w[0, 1], lru_gate_x_w[0, 1]],
                              axis=-1).astype(BF16)
    w_mix = mix_w_out[0].astype(BF16)

    pm = functools.partial(_premix0, gain=norm_mix[0], w_in=w_in_p, q_norm=mla_q_norm[0], w_uq=w_uq_p,
                           kv_norm=mla_kv_norm[0], w_ukv=w_ukv)
    cos, sin_a, sin_b = _rope_tables(t)
    q_l, k_l, v_l, rx_l, ry_l = pm(x, shift=sh1, scale=sc1, cos=cos, sin_a=sin_a, sin_b=sin_b)
    cos, sin_a, sin_b = _identity_rope_tables(tc)
    q_c, k_c, v_c, rx_c, ry_c = pm(ctx, shift=csh1, scale=csc1, cos=cos, sin_a=sin_a, sin_b=sin_b)

    att_l = _mla_attention(q_l, [k_c, k_l], [v_c, v_l], tq=512, n_unit=2).reshape(n_lat, -1)
    att_c = _mla_attention(q_c, [k_c], [v_c], tq=tc, n_unit=1).reshape(n_ctx, -1)
    rec_l, rec_c = _rglru(rx_l, rx_c, ry_l, ry_c, batch, lru_conv_w[0], lru_conv_b[0], w_gates,
                          lru_gate_a_b[0], lru_gate_x_b[0], lru_lambda[0])

    x2, h2 = _mixout(att_l, rec_l, 0, 0, w_mix, x2, g1, norm_ffn[0], sh2, sc2, t)
    c2, hc2 = _mixout(att_c, rec_c, 0, 0, w_mix, c2, cg1, norm_ffn[0], csh2, csc2, tc)

    ffn0 = functools.partial(_ffn, w_up=ffn_w_up[0].astype(BF16), conv_w=ffn_conv_w[0],
                             conv_b=ffn_conv_b[0], w_down=ffn_w_down[0].astype(BF16), final_gain=None)
    x2 = ffn0(x2, h2, gate=g2, seq_len=t, rows_per_mod=t)
    c2 = ffn0(c2, hc2, gate=cg2, seq_len=tc, rows_per_mod=tc)

    (sh1, sc1, g1, sh2, sc2, g2), (csh1, csc1, _, _, _, _) = mod_parts(1)
    w_qkv = na_w_qkv[0].astype(BF16)
    qkv = _premix1(x2, norm_mix[1], sh1, sc1, w_qkv, t, NA_WIDTH)
    kv_c = _premix1(c2, norm_mix[1], csh1, csc1, w_qkv, tc, 0, col0=NA_WIDTH)
    o = _na_attention(qkv, kv_c, _na_bias_table(na_rel_bias[0], t // GRID_W), batch)
    x2, h2 = _mixout(o, o, 0, 1, na_w_out[0].astype(BF16), x2, g1, norm_ffn[1], sh2, sc2, t)
    x2 = _ffn(x2, h2, g2, ffn_w_up[1].astype(BF16), ffn_conv_w[1], ffn_conv_b[1],
              ffn_w_down[1].astype(BF16), final_norm, seq_len=t, rows_per_mod=t)
    return x2.reshape(batch, t, d)
```

```python
import functools

import numpy as np
import jax
import jax.numpy as jnp
from jax import lax
from jax.experimental import pallas as pl
from jax.experimental.pallas import tpu as pltpu

D_MODEL = 2048
DEPTH = 2
GRID_W = 64
NORM_EPS = 1e-6
NEG_INF = -1e30

MLA_HEADS = D_MODEL // 256
MLA_Q_LORA = D_MODEL // 4
MLA_KV_LORA = D_MODEL // 8
MLA_NOPE = 128
MLA_ROPE = 64
MLA_V = 128
MLA_SCALE = (MLA_NOPE + MLA_ROPE) ** -0.5
ROPE_THETA = 10000.0
ROPE_AX_PAIRS = MLA_ROPE // 4

LRU_WIDTH = D_MODEL // 2
LRU_BLOCKS = 8
LRU_BLOCK_W = LRU_WIDTH // LRU_BLOCKS
LRU_CONV = 4
LRU_C = 8.0

NA_HEADS = 16
NA_HEAD_DIM = D_MODEL // NA_HEADS
NA_WIDTH = NA_HEADS * NA_HEAD_DIM
NA_SCALE = NA_HEAD_DIM ** -0.5
NA_ROWS = 8
NA_COLS = 16

FFN_HIDDEN = (D_MODEL * 11) // 4

LANES = 128
SUBLANES = 8
VMEM_BYTES = 64 * 1024 * 1024
ROPE_PAD = LANES
MLA_QK = MLA_NOPE + ROPE_PAD

BF16 = jnp.bfloat16
F32 = jnp.float32


def _compiler_params(semantics, vmem_bytes):
    limit = min(int(vmem_bytes), VMEM_BYTES - 4 * 1024 * 1024)
    return pltpu.CompilerParams(dimension_semantics=semantics, vmem_limit_bytes=limit)


def _resident(shape, index_map):
    return pl.BlockSpec(shape, index_map, pipeline_mode=pl.Buffered(1))


def _rms(x, g):
    return x * lax.rsqrt(jnp.mean(x * x, axis=-1, keepdims=True) + NORM_EPS) * g


def _norm_mod(x, gain, shift, scale):
    return _rms(x, gain) * (1.0 + scale) + shift


def _gelu(x):
    return 0.5 * x * (1.0 + jnp.tanh(0.7978845608028654 * (x + 0.044715 * (x * x * x))))


def _dot(a, b):
    return jnp.dot(a, b, preferred_element_type=F32)


def _dot_nt(a, b):
    return lax.dot_general(a, b, (((1,), (1,)), ((), ())), preferred_element_type=F32)


def _dot_tn(a, b):
    return lax.dot_general(a, b, (((0,), (0,)), ((), ())), preferred_element_type=F32)


def _mod_kernel(c_ref, w_ref, b_ref, o_ref):
    c = c_ref[...]
    s = (c * jax.nn.sigmoid(c)).astype(BF16)
    o_ref[0] = _dot(s, w_ref[0].astype(BF16)) + b_ref[0]


def _modulation(cond, mod_w, mod_b, tn=1024):
    rows = cond.shape[0]
    depth, d, n = mod_w.shape
    return pl.pallas_call(
        _mod_kernel,
        grid=(depth, n // tn),
        in_specs=[
            pl.BlockSpec((rows, d), lambda l, j: (0, 0)),
            pl.BlockSpec((1, d, tn), lambda l, j: (l, 0, j)),
            pl.BlockSpec((1, 1, tn), lambda l, j: (l, 0, j)),
        ],
        out_specs=pl.BlockSpec((1, rows, tn), lambda l, j: (l, 0, j)),
        out_shape=jax.ShapeDtypeStruct((depth, rows, n), F32),
        compiler_params=_compiler_params(("parallel", "parallel"), 40 * 2**20),
        name="modulation",
    )(cond, mod_w, mod_b.reshape(depth, 1, n))


def _rope(x, cos, sin_a, sin_b):
    return x * cos + pltpu.roll(x, LANES - ROPE_AX_PAIRS, 1) * sin_a + pltpu.roll(x, ROPE_AX_PAIRS, 1) * sin_b


def _premix0_kernel(x_ref, gain_ref, sh_ref, sc_ref, win_ref, qn_ref, wuq_ref, kvn_ref, wukv_ref,
                    cos_ref, sina_ref, sinb_ref, q_ref, k_ref, v_ref, rx_ref, ry_ref):
    h = _norm_mod(x_ref[...], gain_ref[...], sh_ref[0], sc_ref[0]).astype(BF16)
    p = _dot(h, win_ref[...])
    o_kv = MLA_Q_LORA
    o_kr = o_kv + MLA_KV_LORA
    o_rx = o_kr + ROPE_PAD
    o_ry = o_rx + LRU_WIDTH
    rx_ref[...] = p[:, o_rx:o_ry]
    ry_ref[...] = p[:, o_ry:]
    cos, sin_a, sin_b = cos_ref[...], sina_ref[...], sinb_ref[...]
    qn = _rms(p[:, :o_kv], qn_ref[...]).astype(BF16)
    q = _dot(qn, wuq_ref[...]) * MLA_SCALE
    cn = _rms(p[:, o_kv:o_kr], kvn_ref[...]).astype(BF16)
    kv = _dot(cn, wukv_ref[...])
    kr = _rope(p[:, o_kr:o_rx], cos, sin_a, sin_b).astype(BF16)
    for hd in range(MLA_HEADS):
        qo = hd * MLA_QK
        q_ref[0, hd, :, :MLA_NOPE] = q[:, qo:qo + MLA_NOPE].astype(BF16)
        q_ref[0, hd, :, MLA_NOPE:] = _rope(q[:, qo + MLA_NOPE:qo + MLA_QK], cos, sin_a, sin_b).astype(BF16)
        ko = hd * (MLA_NOPE + MLA_V)
        k_ref[0, hd, :, :MLA_NOPE] = kv[:, ko:ko + MLA_NOPE].astype(BF16)
        k_ref[0, hd, :, MLA_NOPE:] = kr
        v_ref[0, hd] = kv[:, ko + MLA_NOPE:ko + MLA_NOPE + MLA_V].astype(BF16)


def _premix0(x, gain, shift, scale, w_in, q_norm, w_uq, kv_norm, w_ukv, cos, sin_a, sin_b, tm=256):
    b, t, d = x.shape
    n = b * t
    tpb = t // tm
    nmod = shift.shape[0]
    mod_idx = (lambda i: (i // tpb, 0, 0)) if nmod > 1 else (lambda i: (0, 0, 0))
    const2 = lambda i: (0, 0)
    in_cols = w_in.shape[1]
    head_spec = lambda w: pl.BlockSpec((1, MLA_HEADS, tm, w), lambda i: (i // tpb, 0, i % tpb, 0))
    return pl.pallas_call(
        _premix0_kernel,
        grid=(n // tm,),
        in_specs=[
            pl.BlockSpec((tm, d), lambda i: (i, 0)),
            _resident((1, d), const2),
            pl.BlockSpec((1, 1, d), mod_idx),
            pl.BlockSpec((1, 1, d), mod_idx),
            _resident((d, in_cols), const2),
            _resident((1, MLA_Q_LORA), const2),
            _resident(w_uq.shape, const2),
            _resident((1, MLA_KV_LORA), const2),
            _resident(w_ukv.shape, const2),
            pl.BlockSpec((tm, ROPE_PAD), lambda i: (i % tpb, 0)),
            pl.BlockSpec((tm, ROPE_PAD), lambda i: (i % tpb, 0)),
            pl.BlockSpec((tm, ROPE_PAD), lambda i: (i % tpb, 0)),
        ],
        out_specs=[
            head_spec(MLA_QK), head_spec(MLA_QK), head_spec(MLA_V),
            pl.BlockSpec((tm, LRU_WIDTH), lambda i: (i, 0)),
            pl.BlockSpec((tm, LRU_WIDTH), lambda i: (i, 0)),
        ],
        out_shape=[
            jax.ShapeDtypeStruct((b, MLA_HEADS, t, MLA_QK), BF16),
            jax.ShapeDtypeStruct((b, MLA_HEADS, t, MLA_QK), BF16),
            jax.ShapeDtypeStruct((b, MLA_HEADS, t, MLA_V), BF16),
            jax.ShapeDtypeStruct((n, LRU_WIDTH), F32),
            jax.ShapeDtypeStruct((n, LRU_WIDTH), F32),
        ],
        compiler_params=_compiler_params(("parallel",), 48 * 2**20),
        name="premix0",
    )(x.reshape(n, d), gain.reshape(1, d), shift, scale, w_in, q_norm.reshape(1, -1), w_uq,
      kv_norm.reshape(1, -1), w_ukv, cos, sin_a, sin_b)


def _softmax_pv_t(scores_t, values):
    m = functools.reduce(jnp.maximum, [jnp.max(x, axis=0, keepdims=True) for x in scores_t])
    p = [jnp.exp(x - m) for x in scores_t]
    l = functools.reduce(jnp.add, [jnp.sum(x, axis=0, keepdims=True) for x in p])
    o_t = functools.reduce(jnp.add, [_dot_tn(v, x.astype(BF16)) for x, v in zip(p, values)])
    return (o_t / l).T


def _attn_kernel(*refs, n_seg, tq):
    q_ref = refs[0]
    k_refs = refs[1:1 + n_seg]
    v_refs = refs[1 + n_seg:1 + 2 * n_seg]
    o_ref = refs[1 + 2 * n_seg]
    for r0 in range(0, q_ref.shape[2], tq):
        q = q_ref[0, 0, r0:r0 + tq, :]
        s_t = [_dot_nt(k[0, 0], q) for k in k_refs]
        o_ref[0, r0:r0 + tq, :] = _softmax_pv_t(s_t, [v[0, 0] for v in v_refs]).astype(o_ref.dtype)


def _mla_attention(q, ks, vs, tq, n_unit):
    b, h, t, dk = q.shape
    dv = vs[0].shape[-1]
    n_seg = len(ks)
    tb = tq * n_unit
    kv_spec = lambda a: pl.BlockSpec((1, 1) + a.shape[2:], lambda bi, hi, qi: (bi, hi, 0, 0))
    return pl.pallas_call(
        functools.partial(_attn_kernel, n_seg=n_seg, tq=tq),
        grid=(b, h, t // tb),
        in_specs=[pl.BlockSpec((1, 1, tb, dk), lambda bi, hi, qi: (bi, hi, qi, 0))]
        + [kv_spec(a) for a in ks] + [kv_spec(a) for a in vs],
        out_specs=pl.BlockSpec((1, tb, dv), lambda bi, hi, qi: (bi, qi, hi)),
        out_shape=jax.ShapeDtypeStruct((b, t, h * dv), BF16),
        compiler_params=_compiler_params(("parallel", "parallel", "parallel"), 40 * 2**20),
        name="mla_attention",
    )(q, *ks, *vs)


def _scan_tile(a, b, row, reverse):
    for d in (1, 2, 4):
        if reverse:
            keep = row < SUBLANES - d
            shift = SUBLANES - d
        else:
            keep = row >= d
            shift = d
        a_s = jnp.where(keep, pltpu.roll(a, shift, 0), 1.0)
        b_s = jnp.where(keep, pltpu.roll(b, shift, 0), 0.0)
        b = a * b_s + b
        a = a * a_s
    return a, b


def _rglru_kernel(rxl_ref, rxc_ref, ryl_ref, ryc_ref, cw_ref, cb_ref, wg_ref, gab_ref, gxb_ref, lam_ref,
                  ol_ref, oc_ref, xpad, af, bf, ab, bb, *, chunk):
    t = rxl_ref.shape[0]
    tc = rxc_ref.shape[0]
    width = rxl_ref.shape[1]
    cw = cw_ref[...]
    cb = cb_ref[...]
    z = -lam_ref[...]
    sp = jnp.maximum(z, 0.0) + jnp.log1p(jnp.exp(-jnp.abs(z)))
    gab = gab_ref[...]
    gxb = gxb_ref[...]
    pad = SUBLANES

    def coeffs(x_ref, n, off_f, off_b):
        xpad[0:pad, :] = jnp.zeros((pad, width), F32)
        xpad[pad:pad + n, :] = x_ref[...]
        xpad[pad + n:2 * pad + n, :] = jnp.zeros((pad, width), F32)
        for c0 in range(0, n, chunk):
            xc = cb
            for tap in range(LRU_CONV):
                xc = xc + cw[tap:tap + 1] * xpad[pad - 2 + tap + c0:pad - 2 + tap + c0 + chunk, :]
            xb = xc.astype(BF16)
            for blk in range(width // LRU_BLOCK_W):
                sl = slice(blk * LRU_BLOCK_W, (blk + 1) * LRU_BLOCK_W)
                g = _dot(xb[:, sl], wg_ref[blk])
                xs = xc[:, sl]
                for d, (a_ref, b_ref, off) in enumerate(((af, bf, off_f), (ab, bb, off_b))):
                    g0 = 2 * d * LRU_BLOCK_W
                    r = jax.nn.sigmoid(g[:, g0:g0 + LRU_BLOCK_W] + gab[d:d + 1, sl])
                    i = jax.nn.sigmoid(g[:, g0 + LRU_BLOCK_W:g0 + 2 * LRU_BLOCK_W] + gxb[d:d + 1, sl])
                    log_a = -LRU_C * r * sp[d:d + 1, sl]
                    a = jnp.exp(log_a)
                    one_minus_a2 = -jnp.tanh(log_a) * (a * a + 1.0)
                    a_ref[off + c0:off + c0 + chunk, sl] = a
                    b_ref[off + c0:off + c0 + chunk, sl] = jnp.sqrt(one_minus_a2) * (i * xs)

    coeffs(rxc_ref, tc, 0, t)
    coeffs(rxl_ref, t, tc, 0)

    n_tiles = (t + tc) // SUBLANES
    row = lax.broadcasted_iota(jnp.int32, (SUBLANES, width), 0)

    def body(k, carry):
        hf, hb = carry
        r0 = pl.multiple_of(k * SUBLANES, SUBLANES)
        a, b = _scan_tile(af[pl.ds(r0, SUBLANES), :], bf[pl.ds(r0, SUBLANES), :], row, False)
        h = a * hf + b
        bf[pl.ds(r0, SUBLANES), :] = h
        hf = h[SUBLANES - 1:SUBLANES]
        r1 = pl.multiple_of((n_tiles - 1 - k) * SUBLANES, SUBLANES)
        a, b = _scan_tile(ab[pl.ds(r1, SUBLANES), :], bb[pl.ds(r1, SUBLANES), :], row, True)
        h = a * hb + b
        bb[pl.ds(r1, SUBLANES), :] = h
        hb = h[0:1]
        return hf, hb

    zero = jnp.zeros((1, width), F32)
    lax.fori_loop(0, n_tiles, body, (zero, zero), unroll=4)

    for c0 in range(0, t, chunk):
        rec = bf[tc + c0:tc + c0 + chunk, :] + bb[c0:c0 + chunk, :]
        ol_ref[c0:c0 + chunk, :] = (rec * _gelu(ryl_ref[c0:c0 + chunk, :])).astype(ol_ref.dtype)
    for c0 in range(0, tc, chunk):
        rec = bf[c0:c0 + chunk, :] + bb[t + c0:t + c0 + chunk, :]
        oc_ref[c0:c0 + chunk, :] = (rec * _gelu(ryc_ref[c0:c0 + chunk, :])).astype(oc_ref.dtype)


def _rglru(rx_l, rx_c, ry_l, ry_c, batch, conv_w, conv_b, w_gates, ga_b, gx_b, lam, width=256, chunk=256):
    t = rx_l.shape[0] // batch
    tc = rx_c.shape[0] // batch
    nblk = width // LRU_BLOCK_W
    lat = pl.BlockSpec((t, width), lambda bi, ci: (bi, ci))
    ctx = pl.BlockSpec((tc, width), lambda bi, ci: (bi, ci))
    par = lambda rows: pl.BlockSpec((rows, width), lambda bi, ci: (0, ci))
    seq = t + tc
    return pl.pallas_call(
        functools.partial(_rglru_kernel, chunk=chunk),
        grid=(batch, LRU_WIDTH // width),
        in_specs=[lat, ctx, lat, ctx, par(LRU_CONV), par(1),
                  pl.BlockSpec((nblk,) + w_gates.shape[1:], lambda bi, ci: (ci, 0, 0)),
                  par(2), par(2), par(2)],
        out_specs=[lat, ctx],
        out_shape=[jax.ShapeDtypeStruct(rx_l.shape, BF16), jax.ShapeDtypeStruct(rx_c.shape, BF16)],
        scratch_shapes=[pltpu.VMEM((t + 2 * SUBLANES, width), F32)] + [pltpu.VMEM((seq, width), F32)] * 4,
        compiler_params=_compiler_params(("parallel", "parallel"), 40 * 2**20),
        name="rglru",
    )(rx_l, rx_c, ry_l, ry_c, conv_w, conv_b.reshape(1, -1), w_gates, ga_b, gx_b, lam)


def _mixout_kernel(a_ref, b_ref, wa_ref, wb_ref, x_ref, g_ref, gain_ref, sh_ref, sc_ref, o_ref, h_ref):
    tm = x_ref.shape[0]
    for r0 in range(0, tm, tm // 2):
        rs = slice(r0, r0 + tm // 2)
        y = x_ref[rs, :] + g_ref[0] * (_dot(a_ref[rs, :], wa_ref[...]) + _dot(b_ref[rs, :], wb_ref[...]))
        o_ref[rs, :] = y
        h_ref[rs, :] = _norm_mod(y, gain_ref[...], sh_ref[0], sc_ref[0]).astype(h_ref.dtype)


def _mixout(a, b, a_blk, b_blk, w, x, gate, ffn_gain, ffn_shift, ffn_scale, rows_per_mod, tm=512):
    n, d = x.shape
    kh = w.shape[0] // 2
    nmod = gate.shape[0]
    mod_idx = (lambda i: ((i * tm) // rows_per_mod, 0, 0)) if nmod > 1 else (lambda i: (0, 0, 0))
    row_spec = pl.BlockSpec((tm, d), lambda i: (i, 0))
    mod_spec = pl.BlockSpec((1, 1, d), mod_idx)
    return pl.pallas_call(
        _mixout_kernel,
        grid=(n // tm,),
        in_specs=[
            pl.BlockSpec((tm, kh), lambda i: (i, a_blk)),
            pl.BlockSpec((tm, kh), lambda i: (i, b_blk)),
            _resident((kh, d), lambda i: (0, 0)),
            _resident((kh, d), lambda i: (1, 0)),
            row_spec, mod_spec,
            _resident((1, d), lambda i: (0, 0)),
            mod_spec, mod_spec,
        ],
        out_specs=[row_spec, row_spec],
        out_shape=[jax.ShapeDtypeStruct((n, d), F32), jax.ShapeDtypeStruct((n, d), BF16)],
        compiler_params=_compiler_params(("parallel",), 48 * 2**20),
        name="mixout",
    )(a, b, w, w, x, gate, ffn_gain.reshape(1, d), ffn_shift, ffn_scale)


def _ffn_kernel(h_ref, hp_ref, hn_ref, xc_ref, gt_ref, wgu_ref, cwb_ref,
                wd_ref, fn_ref, o_ref, gpad, act_scr, *y_scr, seq_len, nj, nd):
    i = pl.program_id(0)
    j = pl.program_id(1)
    tm = h_ref.shape[0]
    th = act_scr.shape[2]
    td = wd_ref.shape[3]
    halo = SUBLANES

    @pl.when(j < nj)
    def _():
        h = h_ref[...]
        edge = jnp.concatenate([hp_ref[...].astype(F32)[halo:], hn_ref[...].astype(F32)[:halo]], axis=0)
        g_ext = _dot(jnp.concatenate([h, edge.astype(BF16)], axis=0), wgu_ref[0, 0, :, :th])
        g = g_ext[:tm]
        gpad[halo - 1:halo, :] = g_ext[tm + halo - 1:tm + halo]
        gpad[halo:halo + tm, :] = g
        gpad[halo + tm:halo + tm + 1, :] = g_ext[tm + halo:tm + halo + 1]
        pos = (i * tm + lax.broadcasted_iota(jnp.int32, (tm, 1), 0)) & (seq_len - 1)
        g_prev = jnp.where(pos == 0, 0.0, gpad[halo - 1:halo - 1 + tm, :])
        g_next = jnp.where(pos == seq_len - 1, 0.0, gpad[halo + 1:halo + 1 + tm, :])
        cwb = cwb_ref[0, j]
        conv = cwb[3:4] + cwb[0:1] * g_prev + cwb[1:2] * g + cwb[2:3] * g_next
        act_scr[j] = (_gelu(conv) * _dot(h, wgu_ref[0, 0, :, th:])).astype(BF16)

    @pl.when(j >= nj)
    def _():
        gate = gt_ref[0, j - nj]
        for r0 in range(0, tm, tm // 2):
            rs = slice(r0, r0 + tm // 2)
            acc = _dot(act_scr[0, rs, :], wd_ref[0, 0, 0:th, :])
            for k in range(1, nj):
                acc = acc + _dot(act_scr[k, rs, :], wd_ref[0, 0, k * th:(k + 1) * th, :])
            y = xc_ref[rs, :] + gate * acc
            if y_scr:
                y_scr[0][j - nj, rs, :] = y
            else:
                o_ref[rs, :] = y

    if y_scr:
        @pl.when(j == nj + nd - 1)
        def _():
            ys = y_scr[0]
            ssq = functools.reduce(jnp.add, [jnp.sum(ys[k] * ys[k], axis=-1, keepdims=True) for k in range(nd)])
            inv = lax.rsqrt(ssq / (nd * td) + NORM_EPS)
            for k in range(nd):
                o_ref[:, k * td:(k + 1) * td] = ys[k] * inv * fn_ref[:, k * td:(k + 1) * td]


FFN_TH = 512
FFN_TD = 512


def _ffn_weights(w_up, conv_w, conv_b, w_down):
    depth, d, two_h = w_up.shape
    hidden = two_h // 2
    nj, nd = hidden // FFN_TH, d // FFN_TD
    w_gu = w_up.astype(BF16).reshape(depth, d, 2, nj, FFN_TH).transpose(0, 3, 1, 2, 4).reshape(depth, nj, d, 2 * FFN_TH)
    conv_wb = jnp.concatenate([conv_w, conv_b[:, None, :]], axis=1)
    conv_wb = conv_wb.reshape(depth, conv_wb.shape[1], nj, FFN_TH).transpose(0, 2, 1, 3)
    w_dn = w_down.astype(BF16).reshape(depth, hidden, nd, FFN_TD).transpose(0, 2, 1, 3)
    return w_gu, conv_wb, w_dn


def _ffn(x, h, gate, weights, layer, final_gain, seq_len, rows_per_mod, tm=512):
    w_gu, conv_wb, w_dn = weights
    n, d = x.shape
    _, nj, _, two_th = w_gu.shape
    _, nd, hidden, td = w_dn.shape
    th = two_th // 2
    final_norm = final_gain is not None
    assert seq_len & (seq_len - 1) == 0 and (tm % seq_len == 0 or seq_len % tm == 0)
    nmod = gate.shape[0]
    mod_row = (lambda i: (i * tm) // rows_per_mod) if nmod > 1 else (lambda i: 0)
    edge_rows = 2 * SUBLANES
    bpt = tm // edge_rows
    last_blk = n // edge_rows - 1
    up_col = lambda j: jnp.minimum(j, nj - 1)
    down_col = lambda j: jnp.maximum(j - nj, 0)
    wd_col = lambda j: jnp.where(j == 0, nd - 1, down_col(j))
    if final_norm:
        out_spec = pl.BlockSpec((tm, d), lambda i, j: (i, 0))
        y_scratch = [pltpu.VMEM((nd, tm, td), F32)]
    else:
        out_spec = pl.BlockSpec((tm, td), lambda i, j: (i, down_col(j)))
        y_scratch = []
        final_gain = jnp.zeros((d,), F32)
    return pl.pallas_call(
        functools.partial(_ffn_kernel, seq_len=seq_len, nj=nj, nd=nd),
        grid=(n // tm, nj + nd),
        in_specs=[
            pl.BlockSpec((tm, d), lambda i, j: (i, 0)),
            pl.BlockSpec((edge_rows, d), lambda i, j: (jnp.maximum(i * bpt - 1, 0), 0)),
            pl.BlockSpec((edge_rows, d), lambda i, j: (jnp.minimum((i + 1) * bpt, last_blk), 0)),
            pl.BlockSpec((tm, td), lambda i, j: (i, down_col(j))),
            pl.BlockSpec((1, nd, 1, td), lambda i, j: (mod_row(i), 0, 0, 0)),
            pl.BlockSpec((1, 1, d, two_th), lambda i, j: (layer, up_col(j), 0, 0)),
            _resident((1,) + conv_wb.shape[1:], lambda i, j: (layer, 0, 0, 0)),
            pl.BlockSpec((1, 1, hidden, td), lambda i, j: (layer, wd_col(j), 0, 0)),
            _resident((1, d), lambda i, j: (0, 0)),
        ],
        out_specs=out_spec,
        out_shape=jax.ShapeDtypeStruct((n, d), F32),
        scratch_shapes=[
            pltpu.VMEM((tm + 2 * SUBLANES, th), F32),
            pltpu.VMEM((nj, tm, th), BF16),
        ] + y_scratch,
        compiler_params=_compiler_params(("parallel", "arbitrary"), 56 * 2**20),
        name="conv_ffn",
    )(h, h, h, x, gate.reshape(nmod, nd, 1, td), w_gu, conv_wb, w_dn, final_gain.reshape(1, d))


def _premix1_kernel(x_ref, gain_ref, sh_ref, sc_ref, w_ref, o_ref, h_scr, *, n_q_tiles):
    j = pl.program_id(1)

    @pl.when(j == 0)
    def _():
        h_scr[...] = _norm_mod(x_ref[...], gain_ref[...], sh_ref[0], sc_ref[0]).astype(BF16)

    y = _dot(h_scr[...], w_ref[...])
    if n_q_tiles:
        y = y * jnp.where(j < n_q_tiles, NA_SCALE, 1.0)
    o_ref[...] = y.astype(o_ref.dtype)


def _premix1(x, gain, shift, scale, w, rows_per_mod, n_q_cols, col0=0, tm=1024, tn=1024):
    n, d = x.shape
    cols = w.shape[1] - col0
    j0 = col0 // tn
    nmod = shift.shape[0]
    mod_idx = (lambda i, j: ((i * tm) // rows_per_mod, 0, 0)) if nmod > 1 else (lambda i, j: (0, 0, 0))
    return pl.pallas_call(
        functools.partial(_premix1_kernel, n_q_tiles=n_q_cols // tn),
        grid=(n // tm, cols // tn),
        in_specs=[
            pl.BlockSpec((tm, d), lambda i, j: (i, 0)),
            _resident((1, d), lambda i, j: (0, 0)),
            pl.BlockSpec((1, 1, d), mod_idx),
            pl.BlockSpec((1, 1, d), mod_idx),
            pl.BlockSpec((d, tn), lambda i, j: (0, j + j0)),
        ],
        out_specs=pl.BlockSpec((tm, tn), lambda i, j: (i, j)),
        out_shape=jax.ShapeDtypeStruct((n, cols), BF16),
        scratch_shapes=[pltpu.VMEM((tm, d), BF16)],
        compiler_params=_compiler_params(("parallel", "arbitrary"), 48 * 2**20),
        name="premix1",
    )(x, gain.reshape(1, d), shift, scale, w)


NA_QROWS = 8
NA_BAND = NA_QROWS + NA_ROWS
NA_GROUPS_PER_STEP = 2


def _na_patterns(rows):
    n_groups = rows // NA_QROWS
    assert (n_groups - 2) * NA_QROWS - NA_ROWS // 2 <= rows - NA_BAND
    a = np.arange(NA_QROWS)
    offs, los = [], []
    for g in (0, 1, n_groups - 1):
        r0 = g * NA_QROWS
        u0 = int(np.clip(r0 - NA_ROWS // 2, 0, rows - NA_BAND))
        los.append([int(v) for v in np.clip(r0 + a - NA_ROWS // 2, 0, rows - NA_ROWS) - u0])
        offs.append(u0 - r0 + NA_ROWS - 1)
    pad_lo = max(0, NA_QROWS - 1 - min(offs))
    pad_hi = max(0, max(offs) + NA_BAND - (2 * NA_ROWS - 1))
    return offs, los, pad_lo, pad_hi


def _na_kernel(q_ref, k_ref, v_ref, kc_ref, vc_ref, tbl_ref, o_ref, bm_scr):
    rows = k_ref.shape[1] // GRID_W
    n_groups = rows // NA_QROWS
    tq = NA_QROWS * GRID_W
    tk = NA_BAND * GRID_W

    @pl.when((pl.program_id(1) == 0) & (pl.program_id(2) == 0))
    def _():
        offs, los, pad_lo, _ = _na_patterns(rows)
        c_idx = lax.broadcasted_iota(jnp.int32, (tk, LANES), 0) // GRID_W
        second = lax.broadcasted_iota(jnp.int32, (tk, LANES), 1) >= GRID_W
        for p, (off, lo) in enumerate(zip(offs, los)):
            for ap in range(NA_QROWS // 2):
                start = off - 2 * ap + pad_lo - 1
                blk = tbl_ref[0, start:start + NA_BAND].reshape(tk, LANES)
                lo_l = jnp.where(second, lo[2 * ap + 1], lo[2 * ap])
                ok = (c_idx >= lo_l) & (c_idx < lo_l + NA_ROWS)
                bm_scr[p, :, ap * LANES:(ap + 1) * LANES] = jnp.where(ok, blk, NEG_INF)

    for gg in range(NA_GROUPS_PER_STEP):
        g = pl.program_id(2) * NA_GROUPS_PER_STEP + gg
        u0 = jnp.clip(g * NA_QROWS - NA_ROWS // 2, 0, rows - NA_BAND)
        start = pl.multiple_of(u0 * GRID_W, GRID_W)
        pat = jnp.where(g == 0, 0, jnp.where(g == n_groups - 1, 2, 1))
        qs = slice(gg * tq, (gg + 1) * tq)
        q = q_ref[0, qs, :]
        k_band = k_ref[0, pl.ds(start, tk), :]
        v_band = v_ref[0, pl.ds(start, tk), :]
        s_b = _dot_nt(k_band, q) + bm_scr[pat]
        s_c = _dot_nt(kc_ref[0], q)
        o_ref[0, qs, :] = _softmax_pv_t([s_b, s_c], [v_band, vc_ref[0]]).astype(o_ref.dtype)


def _na_bias_table(rel_bias, rows):
    heads, n_dr, n_dc = rel_bias.shape
    qc = np.arange(GRID_W)[:, None]
    kc = np.arange(GRID_W)[None, :]
    cs = np.clip(qc - NA_COLS // 2, 0, GRID_W - NA_COLS)
    col_ok = (kc >= cs) & (kc < cs + NA_COLS)
    lead = GRID_W - NA_COLS
    ext = jnp.pad(rel_bias.astype(F32), ((0, 0), (0, 0), (lead, lead)))
    width = 2 * GRID_W - 1
    skew = jnp.broadcast_to(ext[:, :, None, :], (heads, n_dr, GRID_W, width))
    skew = jnp.pad(skew, ((0, 0), (0, 0), (0, 0), (0, 1))).reshape(heads, n_dr, GRID_W * (width + 1))
    skew = skew[:, :, :GRID_W * width].reshape(heads, n_dr, GRID_W, width)
    toep = skew[..., GRID_W - 1:]
    toep = jnp.where(col_ok, toep, NEG_INF).transpose(0, 1, 3, 2)
    _, _, pad_lo, pad_hi = _na_patterns(rows)
    toep = jnp.pad(toep, ((0, 0), (pad_lo, pad_hi), (0, 0), (0, 0)))
    return jnp.concatenate([toep[:, 1:], toep[:, :-1]], axis=-1)


def _na_attention(qkv, kv_c, bias_table, batch):
    t = qkv.shape[0] // batch
    tc = kv_c.shape[0] // batch
    qkv = qkv.reshape(batch, t, 3 * NA_WIDTH)
    kv_c = kv_c.reshape(batch, tc, 2 * NA_WIDTH)
    tq = NA_GROUPS_PER_STEP * NA_QROWS * GRID_W
    dh = NA_HEAD_DIM
    out = pl.pallas_call(
        _na_kernel,
        grid=(NA_HEADS, batch, t // tq),
        in_specs=[
            pl.BlockSpec((1, tq, dh), lambda h, b, g: (b, g, h)),
            pl.BlockSpec((1, t, dh), lambda h, b, g: (b, 0, NA_HEADS + h)),
            pl.BlockSpec((1, t, dh), lambda h, b, g: (b, 0, 2 * NA_HEADS + h)),
            pl.BlockSpec((1, tc, dh), lambda h, b, g: (b, 0, h)),
            pl.BlockSpec((1, tc, dh), lambda h, b, g: (b, 0, NA_HEADS + h)),
            pl.BlockSpec((1,) + bias_table.shape[1:], lambda h, b, g: (h, 0, 0, 0)),
        ],
        out_specs=pl.BlockSpec((1, tq, dh), lambda h, b, g: (b, g, h)),
        out_shape=jax.ShapeDtypeStruct((batch, t, NA_WIDTH), BF16),
        scratch_shapes=[pltpu.VMEM((3, NA_BAND * GRID_W, NA_QROWS * GRID_W), F32)],
        compiler_params=_compiler_params(("parallel", "arbitrary", "arbitrary"), 40 * 2**20),
        name="na_attention",
    )(qkv, qkv, qkv, kv_c, kv_c, bias_table)
    return out.reshape(batch * t, NA_WIDTH)


def _rope_tables(t):
    pos = jnp.arange(t)
    inv = ROPE_THETA ** (-jnp.arange(ROPE_AX_PAIRS, dtype=F32) / ROPE_AX_PAIRS)
    ar = (pos // GRID_W).astype(F32)[:, None] * inv
    ac = (pos % GRID_W).astype(F32)[:, None] * inv
    ang = jnp.concatenate([ar, ar, ac, ac], axis=-1)
    cos, sin = jnp.cos(ang), jnp.sin(ang)
    first_half = (jnp.arange(MLA_ROPE) % (2 * ROPE_AX_PAIRS)) < ROPE_AX_PAIRS
    pad = ((0, 0), (0, ROPE_PAD - MLA_ROPE))
    cos_p = jnp.pad(cos, pad, constant_values=1.0)
    sin_a = jnp.pad(jnp.where(first_half, -sin, 0.0), pad)
    sin_b = jnp.pad(jnp.where(first_half, 0.0, sin), pad)
    return cos_p, sin_a, sin_b


def _identity_rope_tables(t):
    return jnp.ones((t, ROPE_PAD), F32), jnp.zeros((t, ROPE_PAD), F32), jnp.zeros((t, ROPE_PAD), F32)


def kernel(x, c, ctx, c_ctx, mod_w, mod_b, norm_mix, norm_ffn, mla_w_in, mla_q_norm, mla_w_uq, mla_kv_norm, mla_w_ukv, lru_conv_w, lru_conv_b, lru_gate_a_w, lru_gate_a_b, lru_gate_x_w, lru_gate_x_b, lru_lambda, mix_w_out, na_w_qkv, na_rel_bias, na_w_out, ffn_w_up, ffn_conv_w, ffn_conv_b, ffn_w_down, final_norm):
    batch, t, d = x.shape
    tc = ctx.shape[1]
    n_lat, n_ctx = batch * t, batch * tc

    cond_rows = 2 * SUBLANES
    cond = jnp.zeros((cond_rows, d), F32).at[:batch].set(c).at[batch].set(c_ctx)
    mod = _modulation(cond, mod_w, mod_b)

    def mod_parts(layer):
        lat = [mod[layer, :batch, k * d:(k + 1) * d].reshape(batch, 1, d) for k in range(6)]
        cx = [mod[layer, batch:batch + 1, k * d:(k + 1) * d].reshape(1, 1, d) for k in range(6)]
        return lat, cx

    x2 = x.reshape(n_lat, d)
    c2 = ctx.reshape(n_ctx, d)

    (sh1, sc1, g1, sh2, sc2, g2), (csh1, csc1, cg1, csh2, csc2, cg2) = mod_parts(0)
    w_in = mla_w_in[0]
    o_kr = MLA_Q_LORA + MLA_KV_LORA + MLA_ROPE
    w_in_p = jnp.concatenate(
        [w_in[:, :o_kr], jnp.zeros((d, ROPE_PAD - MLA_ROPE), w_in.dtype), w_in[:, o_kr:]], axis=1).astype(BF16)
    w_uq_p = jnp.pad(mla_w_uq[0].reshape(MLA_Q_LORA, MLA_HEADS, MLA_NOPE + MLA_ROPE),
                     ((0, 0), (0, 0), (0, ROPE_PAD - MLA_ROPE))).reshape(MLA_Q_LORA, MLA_HEADS * MLA_QK).astype(BF16)
    w_ukv = mla_w_ukv[0].astype(BF16)
    w_gates = jnp.concatenate([lru_gate_a_w[0, 0], lru_gate_x_w[0, 0], lru_gate_a_w[0, 1], lru_gate_x_w[0, 1]],
                              axis=-1).astype(BF16)
    w_mix = mix_w_out[0].astype(BF16)

    pm = functools.partial(_premix0, gain=norm_mix[0], w_in=w_in_p, q_norm=mla_q_norm[0], w_uq=w_uq_p,
                           kv_norm=mla_kv_norm[0], w_ukv=w_ukv)
    cos, sin_a, sin_b = _rope_tables(t)
    q_l, k_l, v_l, rx_l, ry_l = pm(x, shift=sh1, scale=sc1, cos=cos, sin_a=sin_a, sin_b=sin_b)
    cos, sin_a, sin_b = _identity_rope_tables(tc)
    q_c, k_c, v_c, rx_c, ry_c = pm(ctx, shift=csh1, scale=csc1, cos=cos, sin_a=sin_a, sin_b=sin_b)

    att_l = _mla_attention(q_l, [k_c, k_l], [v_c, v_l], tq=512, n_unit=2).reshape(n_lat, -1)
    att_c = _mla_attention(q_c, [k_c], [v_c], tq=tc, n_unit=1).reshape(n_ctx, -1)
    rec_l, rec_c = _rglru(rx_l, rx_c, ry_l, ry_c, batch, lru_conv_w[0], lru_conv_b[0], w_gates,
                          lru_gate_a_b[0], lru_gate_x_b[0], lru_lambda[0])

    x2, h2 = _mixout(att_l, rec_l, 0, 0, w_mix, x2, g1, norm_ffn[0], sh2, sc2, t)
    c2, hc2 = _mixout(att_c, rec_c, 0, 0, w_mix, c2, cg1, norm_ffn[0], csh2, csc2, tc)

    ffn_weights = _ffn_weights(ffn_w_up, ffn_conv_w, ffn_conv_b, ffn_w_down)
    x2 = _ffn(x2, h2, g2, ffn_weights, 0, None, seq_len=t, rows_per_mod=t)
    c2 = _ffn(c2, hc2, cg2, ffn_weights, 0, None, seq_len=tc, rows_per_mod=tc)

    (sh1, sc1, g1, sh2, sc2, g2), (csh1, csc1, _, _, _, _) = mod_parts(1)
    w_qkv = na_w_qkv[0].astype(BF16)
    qkv = _premix1(x2, norm_mix[1], sh1, sc1, w_qkv, t, NA_WIDTH)
    kv_c = _premix1(c2, norm_mix[1], csh1, csc1, w_qkv, tc, 0, col0=NA_WIDTH)
    o = _na_attention(qkv, kv_c, _na_bias_table(na_rel_bias[0], t // GRID_W), batch)
    x2, h2 = _mixout(o, o, 0, 1, na_w_out[0].astype(BF16), x2, g1, norm_ffn[1], sh2, sc2, t)
    x2 = _ffn(x2, h2, g2, ffn_weights, 1, final_norm, seq_len=t, rows_per_mod=t)
    return x2.reshape(batch, t, d)
```

```python
import functools

import numpy as np
import jax
import jax.numpy as jnp
from jax import lax
from jax.experimental import pallas as pl
from jax.experimental.pallas import tpu as pltpu

D_MODEL = 2048
DEPTH = 2
GRID_W = 64
NORM_EPS = 1e-6
NEG_INF = -1e30

MLA_HEADS = D_MODEL // 256
MLA_Q_LORA = D_MODEL // 4
MLA_KV_LORA = D_MODEL // 8
MLA_NOPE = 128
MLA_ROPE = 64
MLA_V = 128
MLA_SCALE = (MLA_NOPE + MLA_ROPE) ** -0.5
ROPE_THETA = 10000.0
ROPE_AX_PAIRS = MLA_ROPE // 4

LRU_WIDTH = D_MODEL // 2
LRU_BLOCKS = 8
LRU_BLOCK_W = LRU_WIDTH // LRU_BLOCKS
LRU_CONV = 4
LRU_C = 8.0

NA_HEADS = 16
NA_HEAD_DIM = D_MODEL // NA_HEADS
NA_WIDTH = NA_HEADS * NA_HEAD_DIM
NA_SCALE = NA_HEAD_DIM ** -0.5
NA_ROWS = 8
NA_COLS = 16

FFN_HIDDEN = (D_MODEL * 11) // 4

LANES = 128
SUBLANES = 8
VMEM_BYTES = 64 * 1024 * 1024
ROPE_PAD = LANES
MLA_QK = MLA_NOPE + ROPE_PAD

BF16 = jnp.bfloat16
F32 = jnp.float32


def _compiler_params(semantics, vmem_bytes):
    limit = min(int(vmem_bytes), VMEM_BYTES - 4 * 1024 * 1024)
    return pltpu.CompilerParams(dimension_semantics=semantics, vmem_limit_bytes=limit)


def _resident(shape, index_map):
    return pl.BlockSpec(shape, index_map, pipeline_mode=pl.Buffered(1))


def _rms(x, g):
    return x * lax.rsqrt(jnp.mean(x * x, axis=-1, keepdims=True) + NORM_EPS) * g


def _norm_mod(x, gain, shift, scale):
    return _rms(x, gain) * (1.0 + scale) + shift


def _gelu(x):
    return 0.5 * x * (1.0 + jnp.tanh(0.7978845608028654 * (x + 0.044715 * (x * x * x))))


def _dot(a, b):
    return jnp.dot(a, b, preferred_element_type=F32)


def _dot_nt(a, b):
    return lax.dot_general(a, b, (((1,), (1,)), ((), ())), preferred_element_type=F32)


def _dot_tn(a, b):
    return lax.dot_general(a, b, (((0,), (0,)), ((), ())), preferred_element_type=F32)


def _mod_kernel(c_ref, w_ref, b_ref, o_ref):
    c = c_ref[...]
    s = (c * jax.nn.sigmoid(c)).astype(BF16)
    o_ref[0] = _dot(s, w_ref[0].astype(BF16)) + b_ref[0]


def _modulation(cond, mod_w, mod_b, tn=1024):
    rows = cond.shape[0]
    depth, d, n = mod_w.shape
    return pl.pallas_call(
        _mod_kernel,
        grid=(depth, n // tn),
        in_specs=[
            pl.BlockSpec((rows, d), lambda l, j: (0, 0)),
            pl.BlockSpec((1, d, tn), lambda l, j: (l, 0, j)),
            pl.BlockSpec((1, 1, tn), lambda l, j: (l, 0, j)),
        ],
        out_specs=pl.BlockSpec((1, rows, tn), lambda l, j: (l, 0, j)),
        out_shape=jax.ShapeDtypeStruct((depth, rows, n), F32),
        compiler_params=_compiler_params(("parallel", "parallel"), 40 * 2**20),
        name="modulation",
    )(cond, mod_w, mod_b.reshape(depth, 1, n))


def _rope(x, cos, sin_a, sin_b):
    return x * cos + pltpu.roll(x, LANES - ROPE_AX_PAIRS, 1) * sin_a + pltpu.roll(x, ROPE_AX_PAIRS, 1) * sin_b


def _premix0_kernel(x_ref, gain_ref, sh_ref, sc_ref, win_ref, qn_ref, wuq_ref, kvn_ref, wukv_ref,
                    cos_ref, sina_ref, sinb_ref, q_ref, k_ref, v_ref, rx_ref, ry_ref):
    h = _norm_mod(x_ref[...], gain_ref[...], sh_ref[0], sc_ref[0]).astype(BF16)
    p = _dot(h, win_ref[...])
    o_kv = MLA_Q_LORA
    o_kr = o_kv + MLA_KV_LORA
    o_rx = o_kr + ROPE_PAD
    o_ry = o_rx + LRU_WIDTH
    rx_ref[...] = p[:, o_rx:o_ry]
    ry_ref[...] = p[:, o_ry:]
    cos, sin_a, sin_b = cos_ref[...], sina_ref[...], sinb_ref[...]
    qn = _rms(p[:, :o_kv], qn_ref[...]).astype(BF16)
    q = _dot(qn, wuq_ref[...]) * MLA_SCALE
    cn = _rms(p[:, o_kv:o_kr], kvn_ref[...]).astype(BF16)
    kv = _dot(cn, wukv_ref[...])
    kr = _rope(p[:, o_kr:o_rx], cos, sin_a, sin_b).astype(BF16)
    for hd in range(MLA_HEADS):
        qo = hd * MLA_QK
        q_ref[0, hd, :, :MLA_NOPE] = q[:, qo:qo + MLA_NOPE].astype(BF16)
        q_ref[0, hd, :, MLA_NOPE:] = _rope(q[:, qo + MLA_NOPE:qo + MLA_QK], cos, sin_a, sin_b).astype(BF16)
        ko = hd * (MLA_NOPE + MLA_V)
        k_ref[0, hd, :, :MLA_NOPE] = kv[:, ko:ko + MLA_NOPE].astype(BF16)
        k_ref[0, hd, :, MLA_NOPE:] = kr
        v_ref[0, hd] = kv[:, ko + MLA_NOPE:ko + MLA_NOPE + MLA_V].astype(BF16)


def _premix0(x, gain, shift, scale, w_in, q_norm, w_uq, kv_norm, w_ukv, cos, sin_a, sin_b, tm=256):
    b, t, d = x.shape
    n = b * t
    tpb = t // tm
    nmod = shift.shape[0]
    mod_idx = (lambda i: (i // tpb, 0, 0)) if nmod > 1 else (lambda i: (0, 0, 0))
    const2 = lambda i: (0, 0)
    in_cols = w_in.shape[1]
    head_spec = lambda w: pl.BlockSpec((1, MLA_HEADS, tm, w), lambda i: (i // tpb, 0, i % tpb, 0))
    return pl.pallas_call(
        _premix0_kernel,
        grid=(n // tm,),
        in_specs=[
            pl.BlockSpec((tm, d), lambda i: (i, 0)),
            _resident((1, d), const2),
            pl.BlockSpec((1, 1, d), mod_idx),
            pl.BlockSpec((1, 1, d), mod_idx),
            _resident((d, in_cols), const2),
            _resident((1, MLA_Q_LORA), const2),
            _resident(w_uq.shape, const2),
            _resident((1, MLA_KV_LORA), const2),
            _resident(w_ukv.shape, const2),
            pl.BlockSpec((tm, ROPE_PAD), lambda i: (i % tpb, 0)),
            pl.BlockSpec((tm, ROPE_PAD), lambda i: (i % tpb, 0)),
            pl.BlockSpec((tm, ROPE_PAD), lambda i: (i % tpb, 0)),
        ],
        out_specs=[
            head_spec(MLA_QK), head_spec(MLA_QK), head_spec(MLA_V),
            pl.BlockSpec((tm, LRU_WIDTH), lambda i: (i, 0)),
            pl.BlockSpec((tm, LRU_WIDTH), lambda i: (i, 0)),
        ],
        out_shape=[
            jax.ShapeDtypeStruct((b, MLA_HEADS, t, MLA_QK), BF16),
            jax.ShapeDtypeStruct((b, MLA_HEADS, t, MLA_QK), BF16),
            jax.ShapeDtypeStruct((b, MLA_HEADS, t, MLA_V), BF16),
            jax.ShapeDtypeStruct((n, LRU_WIDTH), F32),
            jax.ShapeDtypeStruct((n, LRU_WIDTH), F32),
        ],
        compiler_params=_compiler_params(("parallel",), 48 * 2**20),
        name="premix0",
    )(x.reshape(n, d), gain.reshape(1, d), shift, scale, w_in, q_norm.reshape(1, -1), w_uq,
      kv_norm.reshape(1, -1), w_ukv, cos, sin_a, sin_b)


def _softmax_pv_t(scores_t, values):
    m = functools.reduce(jnp.maximum, [jnp.max(x, axis=0, keepdims=True) for x in scores_t])
    p = [jnp.exp(x - m) for x in scores_t]
    l = functools.reduce(jnp.add, [jnp.sum(x, axis=0, keepdims=True) for x in p])
    o_t = functools.reduce(jnp.add, [_dot_tn(v, x.astype(BF16)) for x, v in zip(p, values)])
    return (o_t / l).T


def _attn_kernel(*refs, n_seg, tq):
    q_ref = refs[0]
    k_refs = refs[1:1 + n_seg]
    v_refs = refs[1 + n_seg:1 + 2 * n_seg]
    o_ref = refs[1 + 2 * n_seg]
    for r0 in range(0, q_ref.shape[2], tq):
        q = q_ref[0, 0, r0:r0 + tq, :]
        s_t = [_dot_nt(k[0, 0], q) for k in k_refs]
        o_ref[0, r0:r0 + tq, :] = _softmax_pv_t(s_t, [v[0, 0] for v in v_refs]).astype(o_ref.dtype)


def _mla_attention(q, ks, vs, tq, n_unit):
    b, h, t, dk = q.shape
    dv = vs[0].shape[-1]
    n_seg = len(ks)
    tb = tq * n_unit
    kv_spec = lambda a: pl.BlockSpec((1, 1) + a.shape[2:], lambda bi, hi, qi: (bi, hi, 0, 0))
    return pl.pallas_call(
        functools.partial(_attn_kernel, n_seg=n_seg, tq=tq),
        grid=(b, h, t // tb),
        in_specs=[pl.BlockSpec((1, 1, tb, dk), lambda bi, hi, qi: (bi, hi, qi, 0))]
        + [kv_spec(a) for a in ks] + [kv_spec(a) for a in vs],
        out_specs=pl.BlockSpec((1, tb, dv), lambda bi, hi, qi: (bi, qi, hi)),
        out_shape=jax.ShapeDtypeStruct((b, t, h * dv), BF16),
        compiler_params=_compiler_params(("parallel", "parallel", "parallel"), 40 * 2**20),
        name="mla_attention",
    )(q, *ks, *vs)


def _scan_tile(a, b, row, reverse):
    for d in (1, 2, 4):
        if reverse:
            keep = row < SUBLANES - d
            shift = SUBLANES - d
        else:
            keep = row >= d
            shift = d
        a_s = jnp.where(keep, pltpu.roll(a, shift, 0), 1.0)
        b_s = jnp.where(keep, pltpu.roll(b, shift, 0), 0.0)
        b = a * b_s + b
        a = a * a_s
    return a, b


def _rglru_kernel(rxl_ref, rxc_ref, ryl_ref, ryc_ref, cw_ref, cb_ref, wg_ref, gab_ref, gxb_ref, lam_ref,
                  ol_ref, oc_ref, xpad, af, bf, ab, bb, *, chunk):
    t = rxl_ref.shape[0]
    tc = rxc_ref.shape[0]
    width = rxl_ref.shape[1]
    cw = cw_ref[...]
    cb = cb_ref[...]
    z = -lam_ref[...]
    sp = jnp.maximum(z, 0.0) + jnp.log1p(jnp.exp(-jnp.abs(z)))
    gab = gab_ref[...]
    gxb = gxb_ref[...]
    pad = SUBLANES

    def coeffs(x_ref, n, off_f, off_b):
        xpad[0:pad, :] = jnp.zeros((pad, width), F32)
        xpad[pad:pad + n, :] = x_ref[...]
        xpad[pad + n:2 * pad + n, :] = jnp.zeros((pad, width), F32)
        for c0 in range(0, n, chunk):
            xc = cb
            for tap in range(LRU_CONV):
                xc = xc + cw[tap:tap + 1] * xpad[pad - 2 + tap + c0:pad - 2 + tap + c0 + chunk, :]
            xb = xc.astype(BF16)
            for blk in range(width // LRU_BLOCK_W):
                sl = slice(blk * LRU_BLOCK_W, (blk + 1) * LRU_BLOCK_W)
                g = _dot(xb[:, sl], wg_ref[blk])
                xs = xc[:, sl]
                for d, (a_ref, b_ref, off) in enumerate(((af, bf, off_f), (ab, bb, off_b))):
                    g0 = 2 * d * LRU_BLOCK_W
                    r = jax.nn.sigmoid(g[:, g0:g0 + LRU_BLOCK_W] + gab[d:d + 1, sl])
                    i = jax.nn.sigmoid(g[:, g0 + LRU_BLOCK_W:g0 + 2 * LRU_BLOCK_W] + gxb[d:d + 1, sl])
                    log_a = -LRU_C * r * sp[d:d + 1, sl]
                    a = jnp.exp(log_a)
                    one_minus_a2 = -jnp.tanh(log_a) * (a * a + 1.0)
                    a_ref[off + c0:off + c0 + chunk, sl] = a
                    b_ref[off + c0:off + c0 + chunk, sl] = jnp.sqrt(one_minus_a2) * (i * xs)

    coeffs(rxc_ref, tc, 0, t)
    coeffs(rxl_ref, t, tc, 0)

    n_tiles = (t + tc) // SUBLANES
    row = lax.broadcasted_iota(jnp.int32, (SUBLANES, width), 0)

    def body(k, carry):
        hf, hb = carry
        r0 = pl.multiple_of(k * SUBLANES, SUBLANES)
        a, b = _scan_tile(af[pl.ds(r0, SUBLANES), :], bf[pl.ds(r0, SUBLANES), :], row, False)
        h = a * hf + b
        bf[pl.ds(r0, SUBLANES), :] = h
        hf = h[SUBLANES - 1:SUBLANES]
        r1 = pl.multiple_of((n_tiles - 1 - k) * SUBLANES, SUBLANES)
        a, b = _scan_tile(ab[pl.ds(r1, SUBLANES), :], bb[pl.ds(r1, SUBLANES), :], row, True)
        h = a * hb + b
        bb[pl.ds(r1, SUBLANES), :] = h
        hb = h[0:1]
        return hf, hb

    zero = jnp.zeros((1, width), F32)
    lax.fori_loop(0, n_tiles, body, (zero, zero), unroll=4)

    for c0 in range(0, t, chunk):
        rec = bf[tc + c0:tc + c0 + chunk, :] + bb[c0:c0 + chunk, :]
        ol_ref[c0:c0 + chunk, :] = (rec * _gelu(ryl_ref[c0:c0 + chunk, :])).astype(ol_ref.dtype)
    for c0 in range(0, tc, chunk):
        rec = bf[c0:c0 + chunk, :] + bb[t + c0:t + c0 + chunk, :]
        oc_ref[c0:c0 + chunk, :] = (rec * _gelu(ryc_ref[c0:c0 + chunk, :])).astype(oc_ref.dtype)


def _rglru(rx_l, rx_c, ry_l, ry_c, batch, conv_w, conv_b, w_gates, ga_b, gx_b, lam, width=256, chunk=256):
    t = rx_l.shape[0] // batch
    tc = rx_c.shape[0] // batch
    nblk = width // LRU_BLOCK_W
    lat = pl.BlockSpec((t, width), lambda bi, ci: (bi, ci))
    ctx = pl.BlockSpec((tc, width), lambda bi, ci: (bi, ci))
    par = lambda rows: pl.BlockSpec((rows, width), lambda bi, ci: (0, ci))
    seq = t + tc
    return pl.pallas_call(
        functools.partial(_rglru_kernel, chunk=chunk),
        grid=(batch, LRU_WIDTH // width),
        in_specs=[lat, ctx, lat, ctx, par(LRU_CONV), par(1),
                  pl.BlockSpec((nblk,) + w_gates.shape[1:], lambda bi, ci: (ci, 0, 0)),
                  par(2), par(2), par(2)],
        out_specs=[lat, ctx],
        out_shape=[jax.ShapeDtypeStruct(rx_l.shape, BF16), jax.ShapeDtypeStruct(rx_c.shape, BF16)],
        scratch_shapes=[pltpu.VMEM((t + 2 * SUBLANES, width), F32)] + [pltpu.VMEM((seq, width), F32)] * 4,
        compiler_params=_compiler_params(("parallel", "parallel"), 40 * 2**20),
        name="rglru",
    )(rx_l, rx_c, ry_l, ry_c, conv_w, conv_b.reshape(1, -1), w_gates, ga_b, gx_b, lam)


def _mixout_kernel(a_ref, b_ref, wa_ref, wb_ref, x_ref, g_ref, gain_ref, sh_ref, sc_ref, o_ref, h_ref):
    tm = x_ref.shape[0]
    for r0 in range(0, tm, tm // 2):
        rs = slice(r0, r0 + tm // 2)
        y = x_ref[rs, :] + g_ref[0] * (_dot(a_ref[rs, :], wa_ref[...]) + _dot(b_ref[rs, :], wb_ref[...]))
        o_ref[rs, :] = y
        h_ref[rs, :] = _norm_mod(y, gain_ref[...], sh_ref[0], sc_ref[0]).astype(h_ref.dtype)


def _mixout(a, b, a_blk, b_blk, w, x, gate, ffn_gain, ffn_shift, ffn_scale, rows_per_mod, tm=512):
    n, d = x.shape
    kh = w.shape[0] // 2
    nmod = gate.shape[0]
    mod_idx = (lambda i: ((i * tm) // rows_per_mod, 0, 0)) if nmod > 1 else (lambda i: (0, 0, 0))
    row_spec = pl.BlockSpec((tm, d), lambda i: (i, 0))
    mod_spec = pl.BlockSpec((1, 1, d), mod_idx)
    return pl.pallas_call(
        _mixout_kernel,
        grid=(n // tm,),
        in_specs=[
            pl.BlockSpec((tm, kh), lambda i: (i, a_blk)),
            pl.BlockSpec((tm, kh), lambda i: (i, b_blk)),
            _resident((kh, d), lambda i: (0, 0)),
            _resident((kh, d), lambda i: (1, 0)),
            row_spec, mod_spec,
            _resident((1, d), lambda i: (0, 0)),
            mod_spec, mod_spec,
        ],
        out_specs=[row_spec, row_spec],
        out_shape=[jax.ShapeDtypeStruct((n, d), F32), jax.ShapeDtypeStruct((n, d), BF16)],
        compiler_params=_compiler_params(("parallel",), 48 * 2**20),
        name="mixout",
    )(a, b, w, w, x, gate, ffn_gain.reshape(1, d), ffn_shift, ffn_scale)


def _ffn_kernel(h_ref, hp_ref, hn_ref, xc_ref, gt_ref, wg_ref, wu_ref, cw_ref, cb_ref,
                wd_ref, fn_ref, o_ref, gpad, act_scr, *y_scr, seq_len, nj, nd):
    i = pl.program_id(0)
    j = pl.program_id(1)
    tm = h_ref.shape[0]
    th = act_scr.shape[2]
    td = wd_ref.shape[2]
    halo = SUBLANES

    @pl.when(j < nj)
    def _():
        h = h_ref[...]
        edge = jnp.concatenate([hp_ref[...].astype(F32)[halo:], hn_ref[...].astype(F32)[:halo]], axis=0)
        g_ext = _dot(jnp.concatenate([h, edge.astype(BF16)], axis=0), wg_ref[0])
        g = g_ext[:tm]
        gpad[halo - 1:halo, :] = g_ext[tm + halo - 1:tm + halo]
        gpad[halo:halo + tm, :] = g
        gpad[halo + tm:halo + tm + 1, :] = g_ext[tm + halo:tm + halo + 1]
        pos = (i * tm + lax.broadcasted_iota(jnp.int32, (tm, 1), 0)) & (seq_len - 1)
        g_prev = jnp.where(pos == 0, 0.0, gpad[halo - 1:halo - 1 + tm, :])
        g_next = jnp.where(pos == seq_len - 1, 0.0, gpad[halo + 1:halo + 1 + tm, :])
        cw = cw_ref[...]
        conv = cb_ref[...] + cw[0:1] * g_prev + cw[1:2] * g + cw[2:3] * g_next
        act_scr[j] = (_gelu(conv) * _dot(h, wu_ref[0])).astype(BF16)

    @pl.when(j >= nj)
    def _():
        gate = gt_ref[0]
        for r0 in range(0, tm, tm // 2):
            rs = slice(r0, r0 + tm // 2)
            acc = _dot(act_scr[0, rs, :], wd_ref[0, 0:th, :])
            for k in range(1, nj):
                acc = acc + _dot(act_scr[k, rs, :], wd_ref[0, k * th:(k + 1) * th, :])
            y = xc_ref[rs, :] + gate * acc
            if y_scr:
                y_scr[0][j - nj, rs, :] = y
            else:
                o_ref[rs, :] = y

    if y_scr:
        @pl.when(j == nj + nd - 1)
        def _():
            ys = y_scr[0]
            ssq = functools.reduce(jnp.add, [jnp.sum(ys[k] * ys[k], axis=-1, keepdims=True) for k in range(nd)])
            inv = lax.rsqrt(ssq / (nd * td) + NORM_EPS)
            for k in range(nd):
                o_ref[:, k * td:(k + 1) * td] = ys[k] * inv * fn_ref[:, k * td:(k + 1) * td]


def _ffn(x, h, gate, w_up, w_down, layer, conv_w, conv_b, final_gain, seq_len, rows_per_mod,
         tm=512, th=512, td=512):
    n, d = x.shape
    hidden = w_down.shape[1]
    nj = hidden // th
    nd = d // td
    final_norm = final_gain is not None
    assert seq_len & (seq_len - 1) == 0 and (tm % seq_len == 0 or seq_len % tm == 0)
    nmod = gate.shape[0]
    mod_row = (lambda i: (i * tm) // rows_per_mod) if nmod > 1 else (lambda i: 0)
    edge_rows = 2 * SUBLANES
    bpt = tm // edge_rows
    last_blk = n // edge_rows - 1
    up_col = lambda j: jnp.minimum(j, nj - 1)
    down_col = lambda j: jnp.maximum(j - nj, 0)
    wd_col = lambda j: jnp.where(j == 0, nd - 1, down_col(j))
    if final_norm:
        out_spec = pl.BlockSpec((tm, d), lambda i, j: (i, 0))
        y_scratch = [pltpu.VMEM((nd, tm, td), F32)]
    else:
        out_spec = pl.BlockSpec((tm, td), lambda i, j: (i, down_col(j)))
        y_scratch = []
        final_gain = jnp.zeros((d,), F32)
    return pl.pallas_call(
        functools.partial(_ffn_kernel, seq_len=seq_len, nj=nj, nd=nd),
        grid=(n // tm, nj + nd),
        in_specs=[
            pl.BlockSpec((tm, d), lambda i, j: (i, 0)),
            pl.BlockSpec((edge_rows, d), lambda i, j: (jnp.maximum(i * bpt - 1, 0), 0)),
            pl.BlockSpec((edge_rows, d), lambda i, j: (jnp.minimum((i + 1) * bpt, last_blk), 0)),
            pl.BlockSpec((tm, td), lambda i, j: (i, down_col(j))),
            pl.BlockSpec((1, 1, td), lambda i, j: (mod_row(i), 0, down_col(j))),
            pl.BlockSpec((1, d, th), lambda i, j: (layer, 0, up_col(j))),
            pl.BlockSpec((1, d, th), lambda i, j: (layer, 0, up_col(j) + nj)),
            pl.BlockSpec((conv_w.shape[0], th), lambda i, j: (0, up_col(j))),
            pl.BlockSpec((1, th), lambda i, j: (0, up_col(j))),
            pl.BlockSpec((1, hidden, td), lambda i, j: (layer, 0, wd_col(j))),
            _resident((1, d), lambda i, j: (0, 0)),
        ],
        out_specs=out_spec,
        out_shape=jax.ShapeDtypeStruct((n, d), F32),
        scratch_shapes=[
            pltpu.VMEM((tm + 2 * SUBLANES, th), F32),
            pltpu.VMEM((nj, tm, th), BF16),
        ] + y_scratch,
        compiler_params=_compiler_params(("parallel", "arbitrary"), 56 * 2**20),
        name="conv_ffn",
    )(h, h, h, x, gate, w_up, w_up, conv_w, conv_b.reshape(1, -1), w_down, final_gain.reshape(1, d))


def _premix1_kernel(x_ref, gain_ref, sh_ref, sc_ref, w_ref, o_ref, h_scr, *, n_q_tiles):
    j = pl.program_id(1)

    @pl.when(j == 0)
    def _():
        h_scr[...] = _norm_mod(x_ref[...], gain_ref[...], sh_ref[0], sc_ref[0]).astype(BF16)

    y = _dot(h_scr[...], w_ref[...])
    if n_q_tiles:
        y = y * jnp.where(j < n_q_tiles, NA_SCALE, 1.0)
    o_ref[...] = y.astype(o_ref.dtype)


def _premix1(x, gain, shift, scale, w, rows_per_mod, n_q_cols, col0=0, tm=1024, tn=1024):
    n, d = x.shape
    cols = w.shape[1] - col0
    j0 = col0 // tn
    nmod = shift.shape[0]
    mod_idx = (lambda i, j: ((i * tm) // rows_per_mod, 0, 0)) if nmod > 1 else (lambda i, j: (0, 0, 0))
    return pl.pallas_call(
        functools.partial(_premix1_kernel, n_q_tiles=n_q_cols // tn),
        grid=(n // tm, cols // tn),
        in_specs=[
            pl.BlockSpec((tm, d), lambda i, j: (i, 0)),
            _resident((1, d), lambda i, j: (0, 0)),
            pl.BlockSpec((1, 1, d), mod_idx),
            pl.BlockSpec((1, 1, d), mod_idx),
            pl.BlockSpec((d, tn), lambda i, j: (0, j + j0)),
        ],
        out_specs=pl.BlockSpec((tm, tn), lambda i, j: (i, j)),
        out_shape=jax.ShapeDtypeStruct((n, cols), BF16),
        scratch_shapes=[pltpu.VMEM((tm, d), BF16)],
        compiler_params=_compiler_params(("parallel", "arbitrary"), 48 * 2**20),
        name="premix1",
    )(x, gain.reshape(1, d), shift, scale, w)


NA_QROWS = 8
NA_BAND = NA_QROWS + NA_ROWS
NA_GROUPS_PER_STEP = 4


def _na_patterns(rows):
    n_groups = rows // NA_QROWS
    assert (n_groups - 2) * NA_QROWS - NA_ROWS // 2 <= rows - NA_BAND
    a = np.arange(NA_QROWS)
    offs, los = [], []
    for g in (0, 1, n_groups - 1):
        r0 = g * NA_QROWS
        u0 = int(np.clip(r0 - NA_ROWS // 2, 0, rows - NA_BAND))
        los.append([int(v) for v in np.clip(r0 + a - NA_ROWS // 2, 0, rows - NA_ROWS) - u0])
        offs.append(u0 - r0 + NA_ROWS - 1)
    pad_lo = max(0, NA_QROWS - 1 - min(offs))
    pad_hi = max(0, max(offs) + NA_BAND - (2 * NA_ROWS - 1))
    return offs, los, pad_lo, pad_hi


def _na_kernel(q_ref, k_ref, v_ref, kc_ref, vc_ref, tbl_ref, o_ref, bm_scr):
    rows = k_ref.shape[1] // GRID_W
    n_groups = rows // NA_QROWS
    tq = NA_QROWS * GRID_W
    tk = NA_BAND * GRID_W

    @pl.when((pl.program_id(1) == 0) & (pl.program_id(2) == 0))
    def _():
        offs, los, pad_lo, _ = _na_patterns(rows)
        c_idx = lax.broadcasted_iota(jnp.int32, (tk, LANES), 0) // GRID_W
        second = lax.broadcasted_iota(jnp.int32, (tk, LANES), 1) >= GRID_W
        for p, (off, lo) in enumerate(zip(offs, los)):
            for ap in range(NA_QROWS // 2):
                start = off - 2 * ap + pad_lo - 1
                blk = tbl_ref[0, start:start + NA_BAND].reshape(tk, LANES)
                lo_l = jnp.where(second, lo[2 * ap + 1], lo[2 * ap])
                ok = (c_idx >= lo_l) & (c_idx < lo_l + NA_ROWS)
                bm_scr[p, :, ap * LANES:(ap + 1) * LANES] = jnp.where(ok, blk, NEG_INF)

    for gg in range(NA_GROUPS_PER_STEP):
        g = pl.program_id(2) * NA_GROUPS_PER_STEP + gg
        u0 = jnp.clip(g * NA_QROWS - NA_ROWS // 2, 0, rows - NA_BAND)
        start = pl.multiple_of(u0 * GRID_W, GRID_W)
        pat = jnp.where(g == 0, 0, jnp.where(g == n_groups - 1, 2, 1))
        qs = slice(gg * tq, (gg + 1) * tq)
        q = q_ref[0, qs, :]
        k_band = k_ref[0, pl.ds(start, tk), :]
        v_band = v_ref[0, pl.ds(start, tk), :]
        s_b = _dot_nt(k_band, q) + bm_scr[pat]
        s_c = _dot_nt(kc_ref[0], q)
        o_ref[0, qs, :] = _softmax_pv_t([s_b, s_c], [v_band, vc_ref[0]]).astype(o_ref.dtype)


def _na_bias_table(rel_bias, rows):
    heads, n_dr, n_dc = rel_bias.shape
    qc = np.arange(GRID_W)[:, None]
    kc = np.arange(GRID_W)[None, :]
    cs = np.clip(qc - NA_COLS // 2, 0, GRID_W - NA_COLS)
    col_ok = (kc >= cs) & (kc < cs + NA_COLS)
    lead = GRID_W - NA_COLS
    ext = jnp.pad(rel_bias.astype(F32), ((0, 0), (0, 0), (lead, lead)))
    width = 2 * GRID_W - 1
    skew = jnp.broadcast_to(ext[:, :, None, :], (heads, n_dr, GRID_W, width))
    skew = jnp.pad(skew, ((0, 0), (0, 0), (0, 0), (0, 1))).reshape(heads, n_dr, GRID_W * (width + 1))
    skew = skew[:, :, :GRID_W * width].reshape(heads, n_dr, GRID_W, width)
    toep = skew[..., GRID_W - 1:]
    toep = jnp.where(col_ok, toep, NEG_INF).transpose(0, 1, 3, 2)
    _, _, pad_lo, pad_hi = _na_patterns(rows)
    toep = jnp.pad(toep, ((0, 0), (pad_lo, pad_hi), (0, 0), (0, 0)))
    return jnp.concatenate([toep[:, 1:], toep[:, :-1]], axis=-1)


def _na_attention(qkv, kv_c, bias_table, batch):
    t = qkv.shape[0] // batch
    tc = kv_c.shape[0] // batch
    qkv = qkv.reshape(batch, t, 3 * NA_WIDTH)
    kv_c = kv_c.reshape(batch, tc, 2 * NA_WIDTH)
    tq = NA_GROUPS_PER_STEP * NA_QROWS * GRID_W
    dh = NA_HEAD_DIM
    out = pl.pallas_call(
        _na_kernel,
        grid=(NA_HEADS, batch, t // tq),
        in_specs=[
            pl.BlockSpec((1, tq, dh), lambda h, b, g: (b, g, h)),
            pl.BlockSpec((1, t, dh), lambda h, b, g: (b, 0, NA_HEADS + h)),
            pl.BlockSpec((1, t, dh), lambda h, b, g: (b, 0, 2 * NA_HEADS + h)),
            pl.BlockSpec((1, tc, dh), lambda h, b, g: (b, 0, h)),
            pl.BlockSpec((1, tc, dh), lambda h, b, g: (b, 0, NA_HEADS + h)),
            pl.BlockSpec((1,) + bias_table.shape[1:], lambda h, b, g: (h, 0, 0, 0)),
        ],
        out_specs=pl.BlockSpec((1, tq, dh), lambda h, b, g: (b, g, h)),
        out_shape=jax.ShapeDtypeStruct((batch, t, NA_WIDTH), BF16),
        scratch_shapes=[pltpu.VMEM((3, NA_BAND * GRID_W, NA_QROWS * GRID_W), F32)],
        compiler_params=_compiler_params(("parallel", "arbitrary", "arbitrary"), 40 * 2**20),
        name="na_attention",
    )(qkv, qkv, qkv, kv_c, kv_c, bias_table)
    return out.reshape(batch * t, NA_WIDTH)


def _rope_tables(t):
    pos = jnp.arange(t)
    inv = ROPE_THETA ** (-jnp.arange(ROPE_AX_PAIRS, dtype=F32) / ROPE_AX_PAIRS)
    ar = (pos // GRID_W).astype(F32)[:, None] * inv
    ac = (pos % GRID_W).astype(F32)[:, None] * inv
    ang = jnp.concatenate([ar, ar, ac, ac], axis=-1)
    cos, sin = jnp.cos(ang), jnp.sin(ang)
    first_half = (jnp.arange(MLA_ROPE) % (2 * ROPE_AX_PAIRS)) < ROPE_AX_PAIRS
    pad = ((0, 0), (0, ROPE_PAD - MLA_ROPE))
    cos_p = jnp.pad(cos, pad, constant_values=1.0)
    sin_a = jnp.pad(jnp.where(first_half, -sin, 0.0), pad)
    sin_b = jnp.pad(jnp.where(first_half, 0.0, sin), pad)
    return cos_p, sin_a, sin_b


def _identity_rope_tables(t):
    return jnp.ones((t, ROPE_PAD), F32), jnp.zeros((t, ROPE_PAD), F32), jnp.zeros((t, ROPE_PAD), F32)


def kernel(x, c, ctx, c_ctx, mod_w, mod_b, norm_mix, norm_ffn, mla_w_in, mla_q_norm, mla_w_uq, mla_kv_norm, mla_w_ukv, lru_conv_w, lru_conv_b, lru_gate_a_w, lru_gate_a_b, lru_gate_x_w, lru_gate_x_b, lru_lambda, mix_w_out, na_w_qkv, na_rel_bias, na_w_out, ffn_w_up, ffn_conv_w, ffn_conv_b, ffn_w_down, final_norm):
    batch, t, d = x.shape
    tc = ctx.shape[1]
    n_lat, n_ctx = batch * t, batch * tc

    cond_rows = 2 * SUBLANES
    cond = jnp.zeros((cond_rows, d), F32).at[:batch].set(c).at[batch].set(c_ctx)
    mod = _modulation(cond, mod_w, mod_b)

    def mod_parts(layer):
        lat = [mod[layer, :batch, k * d:(k + 1) * d].reshape(batch, 1, d) for k in range(6)]
        cx = [mod[layer, batch:batch + 1, k * d:(k + 1) * d].reshape(1, 1, d) for k in range(6)]
        return lat, cx

    x2 = x.reshape(n_lat, d)
    c2 = ctx.reshape(n_ctx, d)

    (sh1, sc1, g1, sh2, sc2, g2), (csh1, csc1, cg1, csh2, csc2, cg2) = mod_parts(0)
    w_in = mla_w_in[0]
    o_kr = MLA_Q_LORA + MLA_KV_LORA + MLA_ROPE
    w_in_p = jnp.concatenate(
        [w_in[:, :o_kr], jnp.zeros((d, ROPE_PAD - MLA_ROPE), w_in.dtype), w_in[:, o_kr:]], axis=1).astype(BF16)
    w_uq_p = jnp.pad(mla_w_uq[0].reshape(MLA_Q_LORA, MLA_HEADS, MLA_NOPE + MLA_ROPE),
                     ((0, 0), (0, 0), (0, ROPE_PAD - MLA_ROPE))).reshape(MLA_Q_LORA, MLA_HEADS * MLA_QK).astype(BF16)
    w_ukv = mla_w_ukv[0].astype(BF16)
    w_gates = jnp.concatenate([lru_gate_a_w[0, 0], lru_gate_x_w[0, 0], lru_gate_a_w[0, 1], lru_gate_x_w[0, 1]],
                              axis=-1).astype(BF16)
    w_mix = mix_w_out[0].astype(BF16)

    pm = functools.partial(_premix0, gain=norm_mix[0], w_in=w_in_p, q_norm=mla_q_norm[0], w_uq=w_uq_p,
                           kv_norm=mla_kv_norm[0], w_ukv=w_ukv)
    cos, sin_a, sin_b = _rope_tables(t)
    q_l, k_l, v_l, rx_l, ry_l = pm(x, shift=sh1, scale=sc1, cos=cos, sin_a=sin_a, sin_b=sin_b)
    cos, sin_a, sin_b = _identity_rope_tables(tc)
    q_c, k_c, v_c, rx_c, ry_c = pm(ctx, shift=csh1, scale=csc1, cos=cos, sin_a=sin_a, sin_b=sin_b)

    att_l = _mla_attention(q_l, [k_c, k_l], [v_c, v_l], tq=512, n_unit=4).reshape(n_lat, -1)
    att_c = _mla_attention(q_c, [k_c], [v_c], tq=tc, n_unit=1).reshape(n_ctx, -1)
    rec_l, rec_c = _rglru(rx_l, rx_c, ry_l, ry_c, batch, lru_conv_w[0], lru_conv_b[0], w_gates,
                          lru_gate_a_b[0], lru_gate_x_b[0], lru_lambda[0])

    x2, h2 = _mixout(att_l, rec_l, 0, 0, w_mix, x2, g1, norm_ffn[0], sh2, sc2, t)
    c2, hc2 = _mixout(att_c, rec_c, 0, 0, w_mix, c2, cg1, norm_ffn[0], csh2, csc2, tc)

    w_up_all = ffn_w_up.astype(BF16)
    w_down_all = ffn_w_down.astype(BF16)
    ffn0 = functools.partial(_ffn, w_up=w_up_all, w_down=w_down_all, layer=0, conv_w=ffn_conv_w[0],
                             conv_b=ffn_conv_b[0], final_gain=None)
    x2 = ffn0(x2, h2, g2, seq_len=t, rows_per_mod=t)
    c2 = ffn0(c2, hc2, cg2, seq_len=tc, rows_per_mod=tc)

    (sh1, sc1, g1, sh2, sc2, g2), (csh1, csc1, _, _, _, _) = mod_parts(1)
    w_qkv = na_w_qkv[0].astype(BF16)
    qkv = _premix1(x2, norm_mix[1], sh1, sc1, w_qkv, t, NA_WIDTH)
    kv_c = _premix1(c2, norm_mix[1], csh1, csc1, w_qkv, tc, 0, col0=NA_WIDTH)
    o = _na_attention(qkv, kv_c, _na_bias_table(na_rel_bias[0], t // GRID_W), batch)
    x2, h2 = _mixout(o, o, 0, 1, na_w_out[0].astype(BF16), x2, g1, norm_ffn[1], sh2, sc2, t)
    x2 = _ffn(x2, h2, g2, w_up_all, w_down_all, 1, ffn_conv_w[1], ffn_conv_b[1], final_norm,
              seq_len=t, rows_per_mod=t)
    return x2.reshape(batch, t, d)
```

```python
import functools

import numpy as np
import jax
import jax.numpy as jnp
from jax import lax
from jax.experimental import pallas as pl
from jax.experimental.pallas import tpu as pltpu

D_MODEL = 2048
DEPTH = 2
GRID_W = 64
NORM_EPS = 1e-6
NEG_INF = -1e30

MLA_HEADS = D_MODEL // 256
MLA_Q_LORA = D_MODEL // 4
MLA_KV_LORA = D_MODEL // 8
MLA_NOPE = 128
MLA_ROPE = 64
MLA_V = 128
MLA_SCALE = (MLA_NOPE + MLA_ROPE) ** -0.5
ROPE_THETA = 10000.0
ROPE_AX_PAIRS = MLA_ROPE // 4

LRU_WIDTH = D_MODEL // 2
LRU_BLOCKS = 8
LRU_BLOCK_W = LRU_WIDTH // LRU_BLOCKS
LRU_CONV = 4
LRU_C = 8.0

NA_HEADS = 16
NA_HEAD_DIM = D_MODEL // NA_HEADS
NA_WIDTH = NA_HEADS * NA_HEAD_DIM
NA_SCALE = NA_HEAD_DIM ** -0.5
NA_ROWS = 8
NA_COLS = 16

FFN_HIDDEN = (D_MODEL * 11) // 4

LANES = 128
SUBLANES = 8
VMEM_BYTES = 64 * 1024 * 1024
ROPE_PAD = LANES
MLA_QK = MLA_NOPE + ROPE_PAD

BF16 = jnp.bfloat16
F32 = jnp.float32


def _compiler_params(semantics, vmem_bytes):
    limit = min(int(vmem_bytes), VMEM_BYTES - 4 * 1024 * 1024)
    return pltpu.CompilerParams(dimension_semantics=semantics, vmem_limit_bytes=limit)


def _resident(shape, index_map):
    return pl.BlockSpec(shape, index_map, pipeline_mode=pl.Buffered(1))


def _rms(x, g):
    return x * lax.rsqrt(jnp.mean(x * x, axis=-1, keepdims=True) + NORM_EPS) * g


def _norm_mod(x, gain, shift, scale):
    return _rms(x, gain) * (1.0 + scale) + shift


def _gelu(x):
    return 0.5 * x * (1.0 + jnp.tanh(0.7978845608028654 * (x + 0.044715 * (x * x * x))))


def _dot(a, b):
    return jnp.dot(a, b, preferred_element_type=F32)


def _dot_nt(a, b):
    return lax.dot_general(a, b, (((1,), (1,)), ((), ())), preferred_element_type=F32)


def _dot_tn(a, b):
    return lax.dot_general(a, b, (((0,), (0,)), ((), ())), preferred_element_type=F32)


def _mod_kernel(c_ref, w_ref, b_ref, o_ref):
    c = c_ref[...]
    s = (c * jax.nn.sigmoid(c)).astype(BF16)
    o_ref[0] = _dot(s, w_ref[0].astype(BF16)) + b_ref[0]


def _modulation(cond, mod_w, mod_b, tn=1024):
    rows = cond.shape[0]
    depth, d, n = mod_w.shape
    return pl.pallas_call(
        _mod_kernel,
        grid=(depth, n // tn),
        in_specs=[
            pl.BlockSpec((rows, d), lambda l, j: (0, 0)),
            pl.BlockSpec((1, d, tn), lambda l, j: (l, 0, j)),
            pl.BlockSpec((1, 1, tn), lambda l, j: (l, 0, j)),
        ],
        out_specs=pl.BlockSpec((1, rows, tn), lambda l, j: (l, 0, j)),
        out_shape=jax.ShapeDtypeStruct((depth, rows, n), F32),
        compiler_params=_compiler_params(("parallel", "parallel"), 40 * 2**20),
        name="modulation",
    )(cond, mod_w, mod_b.reshape(depth, 1, n))


def _rope(x, cos, sin_a, sin_b):
    return x * cos + pltpu.roll(x, LANES - ROPE_AX_PAIRS, 1) * sin_a + pltpu.roll(x, ROPE_AX_PAIRS, 1) * sin_b


def _premix0_kernel(x_ref, gain_ref, sh_ref, sc_ref, win_ref, qn_ref, wuq_ref, kvn_ref, wukv_ref,
                    cos_ref, sina_ref, sinb_ref, q_ref, k_ref, v_ref, rx_ref, ry_ref):
    h = _norm_mod(x_ref[...], gain_ref[...], sh_ref[0], sc_ref[0]).astype(BF16)
    p = _dot(h, win_ref[...])
    o_kv = MLA_Q_LORA
    o_kr = o_kv + MLA_KV_LORA
    o_rx = o_kr + ROPE_PAD
    o_ry = o_rx + LRU_WIDTH
    rx_ref[...] = p[:, o_rx:o_ry]
    ry_ref[...] = p[:, o_ry:]
    cos, sin_a, sin_b = cos_ref[...], sina_ref[...], sinb_ref[...]
    qn = _rms(p[:, :o_kv], qn_ref[...]).astype(BF16)
    q = _dot(qn, wuq_ref[...]) * MLA_SCALE
    cn = _rms(p[:, o_kv:o_kr], kvn_ref[...]).astype(BF16)
    kv = _dot(cn, wukv_ref[...])
    kr = _rope(p[:, o_kr:o_rx], cos, sin_a, sin_b).astype(BF16)
    for hd in range(MLA_HEADS):
        qo = hd * MLA_QK
        q_ref[0, hd, :, :MLA_NOPE] = q[:, qo:qo + MLA_NOPE].astype(BF16)
        q_ref[0, hd, :, MLA_NOPE:] = _rope(q[:, qo + MLA_NOPE:qo + MLA_QK], cos, sin_a, sin_b).astype(BF16)
        ko = hd * (MLA_NOPE + MLA_V)
        k_ref[0, hd, :, :MLA_NOPE] = kv[:, ko:ko + MLA_NOPE].astype(BF16)
        k_ref[0, hd, :, MLA_NOPE:] = kr
        v_ref[0, hd] = kv[:, ko + MLA_NOPE:ko + MLA_NOPE + MLA_V].astype(BF16)


def _premix0(x, gain, shift, scale, w_in, q_norm, w_uq, kv_norm, w_ukv, cos, sin_a, sin_b, tm=256):
    b, t, d = x.shape
    n = b * t
    tpb = t // tm
    nmod = shift.shape[0]
    mod_idx = (lambda i: (i // tpb, 0, 0)) if nmod > 1 else (lambda i: (0, 0, 0))
    const2 = lambda i: (0, 0)
    in_cols = w_in.shape[1]
    head_spec = lambda w: pl.BlockSpec((1, MLA_HEADS, tm, w), lambda i: (i // tpb, 0, i % tpb, 0))
    return pl.pallas_call(
        _premix0_kernel,
        grid=(n // tm,),
        in_specs=[
            pl.BlockSpec((tm, d), lambda i: (i, 0)),
            _resident((1, d), const2),
            pl.BlockSpec((1, 1, d), mod_idx),
            pl.BlockSpec((1, 1, d), mod_idx),
            _resident((d, in_cols), const2),
            _resident((1, MLA_Q_LORA), const2),
            _resident(w_uq.shape, const2),
            _resident((1, MLA_KV_LORA), const2),
            _resident(w_ukv.shape, const2),
            pl.BlockSpec((tm, ROPE_PAD), lambda i: (i % tpb, 0)),
            pl.BlockSpec((tm, ROPE_PAD), lambda i: (i % tpb, 0)),
            pl.BlockSpec((tm, ROPE_PAD), lambda i: (i % tpb, 0)),
        ],
        out_specs=[
            head_spec(MLA_QK), head_spec(MLA_QK), head_spec(MLA_V),
            pl.BlockSpec((tm, LRU_WIDTH), lambda i: (i, 0)),
            pl.BlockSpec((tm, LRU_WIDTH), lambda i: (i, 0)),
        ],
        out_shape=[
            jax.ShapeDtypeStruct((b, MLA_HEADS, t, MLA_QK), BF16),
            jax.ShapeDtypeStruct((b, MLA_HEADS, t, MLA_QK), BF16),
            jax.ShapeDtypeStruct((b, MLA_HEADS, t, MLA_V), BF16),
            jax.ShapeDtypeStruct((n, LRU_WIDTH), F32),
            jax.ShapeDtypeStruct((n, LRU_WIDTH), F32),
        ],
        compiler_params=_compiler_params(("parallel",), 48 * 2**20),
        name="premix0",
    )(x.reshape(n, d), gain.reshape(1, d), shift, scale, w_in, q_norm.reshape(1, -1), w_uq,
      kv_norm.reshape(1, -1), w_ukv, cos, sin_a, sin_b)


def _softmax_pv_t(scores_t, values):
    m = functools.reduce(jnp.maximum, [jnp.max(x, axis=0, keepdims=True) for x in scores_t])
    p = [jnp.exp(x - m) for x in scores_t]
    l = functools.reduce(jnp.add, [jnp.sum(x, axis=0, keepdims=True) for x in p])
    o_t = functools.reduce(jnp.add, [_dot_tn(v, x.astype(BF16)) for x, v in zip(p, values)])
    return (o_t / l).T


def _attn_kernel(*refs, n_seg, tq):
    q_ref = refs[0]
    k_refs = refs[1:1 + n_seg]
    v_refs = refs[1 + n_seg:1 + 2 * n_seg]
    o_ref = refs[1 + 2 * n_seg]
    s_scr = refs[2 + 2 * n_seg]
    n_unit = q_ref.shape[2] // tq
    seg_rows = []
    row = 0
    for k in k_refs:
        seg_rows.append(slice(row, row + k.shape[2]))
        row += k.shape[2]

    def scores(u):
        q = q_ref[0, 0, u * tq:(u + 1) * tq, :]
        for k, rows in zip(k_refs, seg_rows):
            s_scr[u % 2, rows, :] = _dot_nt(k[0, 0], q)

    def finish(u):
        s_t = [s_scr[u % 2, rows, :] for rows in seg_rows]
        o_ref[0, u * tq:(u + 1) * tq, :] = _softmax_pv_t(s_t, [v[0, 0] for v in v_refs]).astype(o_ref.dtype)

    scores(0)
    for u in range(1, n_unit):
        scores(u)
        finish(u - 1)
    finish(n_unit - 1)


def _mla_attention(q, ks, vs, tq, n_unit):
    b, h, t, dk = q.shape
    dv = vs[0].shape[-1]
    n_seg = len(ks)
    tb = tq * n_unit
    kv_spec = lambda a: pl.BlockSpec((1, 1) + a.shape[2:], lambda bi, hi, qi: (bi, hi, 0, 0))
    return pl.pallas_call(
        functools.partial(_attn_kernel, n_seg=n_seg, tq=tq),
        grid=(b, h, t // tb),
        in_specs=[pl.BlockSpec((1, 1, tb, dk), lambda bi, hi, qi: (bi, hi, qi, 0))]
        + [kv_spec(a) for a in ks] + [kv_spec(a) for a in vs],
        out_specs=pl.BlockSpec((1, tb, dv), lambda bi, hi, qi: (bi, qi, hi)),
        out_shape=jax.ShapeDtypeStruct((b, t, h * dv), BF16),
        scratch_shapes=[pltpu.VMEM((2, sum(a.shape[2] for a in ks), tq), F32)],
        compiler_params=_compiler_params(("parallel", "parallel", "parallel"), 40 * 2**20),
        name="mla_attention",
    )(q, *ks, *vs)


def _scan_tile(a, b, row, reverse):
    for d in (1, 2, 4):
        if reverse:
            keep = row < SUBLANES - d
            shift = SUBLANES - d
        else:
            keep = row >= d
            shift = d
        a_s = jnp.where(keep, pltpu.roll(a, shift, 0), 1.0)
        b_s = jnp.where(keep, pltpu.roll(b, shift, 0), 0.0)
        b = a * b_s + b
        a = a * a_s
    return a, b


def _rglru_kernel(rxl_ref, rxc_ref, ryl_ref, ryc_ref, cw_ref, cb_ref, wg_ref, gab_ref, gxb_ref, lam_ref,
                  ol_ref, oc_ref, xpad, af, bf, ab, bb, *, chunk):
    t = rxl_ref.shape[0]
    tc = rxc_ref.shape[0]
    width = rxl_ref.shape[1]
    cw = cw_ref[...]
    cb = cb_ref[...]
    z = -lam_ref[...]
    sp = jnp.maximum(z, 0.0) + jnp.log1p(jnp.exp(-jnp.abs(z)))
    gab = gab_ref[...]
    gxb = gxb_ref[...]
    pad = SUBLANES

    def coeffs(x_ref, n, off_f, off_b):
        xpad[0:pad, :] = jnp.zeros((pad, width), F32)
        xpad[pad:pad + n, :] = x_ref[...]
        xpad[pad + n:2 * pad + n, :] = jnp.zeros((pad, width), F32)
        for c0 in range(0, n, chunk):
            xc = cb
            for tap in range(LRU_CONV):
                xc = xc + cw[tap:tap + 1] * xpad[pad - 2 + tap + c0:pad - 2 + tap + c0 + chunk, :]
            xb = xc.astype(BF16)
            for blk in range(width // LRU_BLOCK_W):
                sl = slice(blk * LRU_BLOCK_W, (blk + 1) * LRU_BLOCK_W)
                g = _dot(xb[:, sl], wg_ref[blk])
                xs = xc[:, sl]
                for d, (a_ref, b_ref, off) in enumerate(((af, bf, off_f), (ab, bb, off_b))):
                    g0 = 2 * d * LRU_BLOCK_W
                    r = jax.nn.sigmoid(g[:, g0:g0 + LRU_BLOCK_W] + gab[d:d + 1, sl])
                    i = jax.nn.sigmoid(g[:, g0 + LRU_BLOCK_W:g0 + 2 * LRU_BLOCK_W] + gxb[d:d + 1, sl])
                    log_a = -LRU_C * r * sp[d:d + 1, sl]
                    a = jnp.exp(log_a)
                    one_minus_a2 = -jnp.tanh(log_a) * (a * a + 1.0)
                    a_ref[off + c0:off + c0 + chunk, sl] = a
                    b_ref[off + c0:off + c0 + chunk, sl] = jnp.sqrt(one_minus_a2) * (i * xs)

    coeffs(rxc_ref, tc, 0, t)
    coeffs(rxl_ref, t, tc, 0)

    n_tiles = (t + tc) // SUBLANES
    row = lax.broadcasted_iota(jnp.int32, (SUBLANES, width), 0)

    def body(k, carry):
        hf, hb = carry
        r0 = pl.multiple_of(k * SUBLANES, SUBLANES)
        a, b = _scan_tile(af[pl.ds(r0, SUBLANES), :], bf[pl.ds(r0, SUBLANES), :], row, False)
        h = a * hf + b
        bf[pl.ds(r0, SUBLANES), :] = h
        hf = h[SUBLANES - 1:SUBLANES]
        r1 = pl.multiple_of((n_tiles - 1 - k) * SUBLANES, SUBLANES)
        a, b = _scan_tile(ab[pl.ds(r1, SUBLANES), :], bb[pl.ds(r1, SUBLANES), :], row, True)
        h = a * hb + b
        bb[pl.ds(r1, SUBLANES), :] = h
        hb = h[0:1]
        return hf, hb

    zero = jnp.zeros((1, width), F32)
    lax.fori_loop(0, n_tiles, body, (zero, zero), unroll=4)

    for c0 in range(0, t, chunk):
        rec = bf[tc + c0:tc + c0 + chunk, :] + bb[c0:c0 + chunk, :]
        ol_ref[c0:c0 + chunk, :] = (rec * _gelu(ryl_ref[c0:c0 + chunk, :])).astype(ol_ref.dtype)
    for c0 in range(0, tc, chunk):
        rec = bf[c0:c0 + chunk, :] + bb[t + c0:t + c0 + chunk, :]
        oc_ref[c0:c0 + chunk, :] = (rec * _gelu(ryc_ref[c0:c0 + chunk, :])).astype(oc_ref.dtype)


def _rglru(rx_l, rx_c, ry_l, ry_c, batch, conv_w, conv_b, w_gates, ga_b, gx_b, lam, width=256, chunk=256):
    t = rx_l.shape[0] // batch
    tc = rx_c.shape[0] // batch
    nblk = width // LRU_BLOCK_W
    lat = pl.BlockSpec((t, width), lambda bi, ci: (bi, ci))
    ctx = pl.BlockSpec((tc, width), lambda bi, ci: (bi, ci))
    par = lambda rows: pl.BlockSpec((rows, width), lambda bi, ci: (0, ci))
    seq = t + tc
    return pl.pallas_call(
        functools.partial(_rglru_kernel, chunk=chunk),
        grid=(batch, LRU_WIDTH // width),
        in_specs=[lat, ctx, lat, ctx, par(LRU_CONV), par(1),
                  pl.BlockSpec((nblk,) + w_gates.shape[1:], lambda bi, ci: (ci, 0, 0)),
                  par(2), par(2), par(2)],
        out_specs=[lat, ctx],
        out_shape=[jax.ShapeDtypeStruct(rx_l.shape, BF16), jax.ShapeDtypeStruct(rx_c.shape, BF16)],
        scratch_shapes=[pltpu.VMEM((t + 2 * SUBLANES, width), F32)] + [pltpu.VMEM((seq, width), F32)] * 4,
        compiler_params=_compiler_params(("parallel", "parallel"), 40 * 2**20),
        name="rglru",
    )(rx_l, rx_c, ry_l, ry_c, conv_w, conv_b.reshape(1, -1), w_gates, ga_b, gx_b, lam)


def _mixout_kernel(a_ref, b_ref, wa_ref, wb_ref, x_ref, g_ref, gain_ref, sh_ref, sc_ref, o_ref, h_ref):
    tm = x_ref.shape[0]
    for r0 in range(0, tm, tm // 2):
        rs = slice(r0, r0 + tm // 2)
        y = x_ref[rs, :] + g_ref[0] * (_dot(a_ref[rs, :], wa_ref[...]) + _dot(b_ref[rs, :], wb_ref[...]))
        o_ref[rs, :] = y
        h_ref[rs, :] = _norm_mod(y, gain_ref[...], sh_ref[0], sc_ref[0]).astype(h_ref.dtype)


def _mixout(a, b, a_blk, b_blk, w, x, gate, ffn_gain, ffn_shift, ffn_scale, rows_per_mod, tm=512):
    n, d = x.shape
    kh = w.shape[0] // 2
    nmod = gate.shape[0]
    mod_idx = (lambda i: ((i * tm) // rows_per_mod, 0, 0)) if nmod > 1 else (lambda i: (0, 0, 0))
    row_spec = pl.BlockSpec((tm, d), lambda i: (i, 0))
    mod_spec = pl.BlockSpec((1, 1, d), mod_idx)
    return pl.pallas_call(
        _mixout_kernel,
        grid=(n // tm,),
        in_specs=[
            pl.BlockSpec((tm, kh), lambda i: (i, a_blk)),
            pl.BlockSpec((tm, kh), lambda i: (i, b_blk)),
            _resident((kh, d), lambda i: (0, 0)),
            _resident((kh, d), lambda i: (1, 0)),
            row_spec, mod_spec,
            _resident((1, d), lambda i: (0, 0)),
            mod_spec, mod_spec,
        ],
        out_specs=[row_spec, row_spec],
        out_shape=[jax.ShapeDtypeStruct((n, d), F32), jax.ShapeDtypeStruct((n, d), BF16)],
        compiler_params=_compiler_params(("parallel",), 48 * 2**20),
        name="mixout",
    )(a, b, w, w, x, gate, ffn_gain.reshape(1, d), ffn_shift, ffn_scale)


def _ffn_kernel(h_ref, hp_ref, hn_ref, xc_ref, gt_ref, wg_ref, wu_ref, cw_ref, cb_ref,
                wd_ref, fn_ref, o_ref, gpad, act_scr, *y_scr, seq_len, nj, nd):
    i = pl.program_id(0)
    j = pl.program_id(1)
    tm = h_ref.shape[0]
    th = act_scr.shape[2]
    td = wd_ref.shape[2]
    halo = SUBLANES

    @pl.when(j < nj)
    def _():
        h = h_ref[...]
        edge = jnp.concatenate([hp_ref[...].astype(F32)[halo:], hn_ref[...].astype(F32)[:halo]], axis=0)
        g_ext = _dot(jnp.concatenate([h, edge.astype(BF16)], axis=0), wg_ref[0])
        g = g_ext[:tm]
        gpad[halo - 1:halo, :] = g_ext[tm + halo - 1:tm + halo]
        gpad[halo:halo + tm, :] = g
        gpad[halo + tm:halo + tm + 1, :] = g_ext[tm + halo:tm + halo + 1]
        pos = (i * tm + lax.broadcasted_iota(jnp.int32, (tm, 1), 0)) & (seq_len - 1)
        g_prev = jnp.where(pos == 0, 0.0, gpad[halo - 1:halo - 1 + tm, :])
        g_next = jnp.where(pos == seq_len - 1, 0.0, gpad[halo + 1:halo + 1 + tm, :])
        cw = cw_ref[...]
        conv = cb_ref[...] + cw[0:1] * g_prev + cw[1:2] * g + cw[2:3] * g_next
        act_scr[j] = (_gelu(conv) * _dot(h, wu_ref[0])).astype(BF16)

    @pl.when(j >= nj)
    def _():
        gate = gt_ref[0]
        for r0 in range(0, tm, tm // 2):
            rs = slice(r0, r0 + tm // 2)
            acc = _dot(act_scr[0, rs, :], wd_ref[0, 0:th, :])
            for k in range(1, nj):
                acc = acc + _dot(act_scr[k, rs, :], wd_ref[0, k * th:(k + 1) * th, :])
            y = xc_ref[rs, :] + gate * acc
            if y_scr:
                y_scr[0][j - nj, rs, :] = y
            else:
                o_ref[rs, :] = y

    if y_scr:
        @pl.when(j == nj + nd - 1)
        def _():
            ys = y_scr[0]
            ssq = functools.reduce(jnp.add, [jnp.sum(ys[k] * ys[k], axis=-1, keepdims=True) for k in range(nd)])
            inv = lax.rsqrt(ssq / (nd * td) + NORM_EPS)
            for k in range(nd):
                o_ref[:, k * td:(k + 1) * td] = ys[k] * inv * fn_ref[:, k * td:(k + 1) * td]


def _ffn(x, h, gate, w_up, w_down, layer, conv_w, conv_b, final_gain, seq_len, rows_per_mod,
         tm=512, th=512, td=512):
    n, d = x.shape
    hidden = w_down.shape[1]
    nj = hidden // th
    nd = d // td
    final_norm = final_gain is not None
    assert seq_len & (seq_len - 1) == 0 and (tm % seq_len == 0 or seq_len % tm == 0)
    nmod = gate.shape[0]
    mod_row = (lambda i: (i * tm) // rows_per_mod) if nmod > 1 else (lambda i: 0)
    edge_rows = 2 * SUBLANES
    bpt = tm // edge_rows
    last_blk = n // edge_rows - 1
    up_col = lambda j: jnp.minimum(j, nj - 1)
    down_col = lambda j: jnp.maximum(j - nj, 0)
    wd_col = lambda j: jnp.where(j == 0, nd - 1, down_col(j))
    if final_norm:
        out_spec = pl.BlockSpec((tm, d), lambda i, j: (i, 0))
        y_scratch = [pltpu.VMEM((nd, tm, td), F32)]
    else:
        out_spec = pl.BlockSpec((tm, td), lambda i, j: (i, down_col(j)))
        y_scratch = []
        final_gain = jnp.zeros((d,), F32)
    return pl.pallas_call(
        functools.partial(_ffn_kernel, seq_len=seq_len, nj=nj, nd=nd),
        grid=(n // tm, nj + nd),
        in_specs=[
            pl.BlockSpec((tm, d), lambda i, j: (i, 0)),
            pl.BlockSpec((edge_rows, d), lambda i, j: (jnp.maximum(i * bpt - 1, 0), 0)),
            pl.BlockSpec((edge_rows, d), lambda i, j: (jnp.minimum((i + 1) * bpt, last_blk), 0)),
            pl.BlockSpec((tm, td), lambda i, j: (i, down_col(j))),
            pl.BlockSpec((1, 1, td), lambda i, j: (mod_row(i), 0, down_col(j))),
            pl.BlockSpec((1, d, th), lambda i, j: (layer, 0, up_col(j))),
            pl.BlockSpec((1, d, th), lambda i, j: (layer, 0, up_col(j) + nj)),
            pl.BlockSpec((conv_w.shape[0], th), lambda i, j: (0, up_col(j))),
            pl.BlockSpec((1, th), lambda i, j: (0, up_col(j))),
            pl.BlockSpec((1, hidden, td), lambda i, j: (layer, 0, wd_col(j))),
            _resident((1, d), lambda i, j: (0, 0)),
        ],
        out_specs=out_spec,
        out_shape=jax.ShapeDtypeStruct((n, d), F32),
        scratch_shapes=[
            pltpu.VMEM((tm + 2 * SUBLANES, th), F32),
            pltpu.VMEM((nj, tm, th), BF16),
        ] + y_scratch,
        compiler_params=_compiler_params(("parallel", "arbitrary"), 56 * 2**20),
        name="conv_ffn",
    )(h, h, h, x, gate, w_up, w_up, conv_w, conv_b.reshape(1, -1), w_down, final_gain.reshape(1, d))


def _premix1_kernel(x_ref, gain_ref, sh_ref, sc_ref, w_ref, o_ref, h_scr, *, n_q_tiles):
    j = pl.program_id(1)

    @pl.when(j == 0)
    def _():
        h_scr[...] = _norm_mod(x_ref[...], gain_ref[...], sh_ref[0], sc_ref[0]).astype(BF16)

    y = _dot(h_scr[...], w_ref[...])
    if n_q_tiles:
        y = y * jnp.where(j < n_q_tiles, NA_SCALE, 1.0)
    o_ref[...] = y.astype(o_ref.dtype)


def _premix1(x, gain, shift, scale, w, rows_per_mod, n_q_cols, col0=0, tm=1024, tn=1024):
    n, d = x.shape
    cols = w.shape[1] - col0
    j0 = col0 // tn
    nmod = shift.shape[0]
    mod_idx = (lambda i, j: ((i * tm) // rows_per_mod, 0, 0)) if nmod > 1 else (lambda i, j: (0, 0, 0))
    return pl.pallas_call(
        functools.partial(_premix1_kernel, n_q_tiles=n_q_cols // tn),
        grid=(n // tm, cols // tn),
        in_specs=[
            pl.BlockSpec((tm, d), lambda i, j: (i, 0)),
            _resident((1, d), lambda i, j: (0, 0)),
            pl.BlockSpec((1, 1, d), mod_idx),
            pl.BlockSpec((1, 1, d), mod_idx),
            pl.BlockSpec((d, tn), lambda i, j: (0, j + j0)),
        ],
        out_specs=pl.BlockSpec((tm, tn), lambda i, j: (i, j)),
        out_shape=jax.ShapeDtypeStruct((n, cols), BF16),
        scratch_shapes=[pltpu.VMEM((tm, d), BF16)],
        compiler_params=_compiler_params(("parallel", "arbitrary"), 48 * 2**20),
        name="premix1",
    )(x, gain.reshape(1, d), shift, scale, w)


NA_QROWS = 8
NA_BAND = NA_QROWS + NA_ROWS
NA_GROUPS_PER_STEP = 4


def _na_patterns(rows):
    n_groups = rows // NA_QROWS
    assert (n_groups - 2) * NA_QROWS - NA_ROWS // 2 <= rows - NA_BAND
    a = np.arange(NA_QROWS)
    offs, los = [], []
    for g in (0, 1, n_groups - 1):
        r0 = g * NA_QROWS
        u0 = int(np.clip(r0 - NA_ROWS // 2, 0, rows - NA_BAND))
        los.append([int(v) for v in np.clip(r0 + a - NA_ROWS // 2, 0, rows - NA_ROWS) - u0])
        offs.append(u0 - r0 + NA_ROWS - 1)
    pad_lo = max(0, NA_QROWS - 1 - min(offs))
    pad_hi = max(0, max(offs) + NA_BAND - (2 * NA_ROWS - 1))
    return offs, los, pad_lo, pad_hi


def _na_kernel(q_ref, k_ref, v_ref, kc_ref, vc_ref, tbl_ref, o_ref, bm_scr, s_scr):
    rows = k_ref.shape[1] // GRID_W
    n_groups = rows // NA_QROWS
    tq = NA_QROWS * GRID_W
    tk = NA_BAND * GRID_W

    @pl.when((pl.program_id(1) == 0) & (pl.program_id(2) == 0))
    def _():
        offs, los, pad_lo, _ = _na_patterns(rows)
        c_idx = lax.broadcasted_iota(jnp.int32, (tk, LANES), 0) // GRID_W
        second = lax.broadcasted_iota(jnp.int32, (tk, LANES), 1) >= GRID_W
        for p, (off, lo) in enumerate(zip(offs, los)):
            for ap in range(NA_QROWS // 2):
                start = off - 2 * ap + pad_lo - 1
                blk = tbl_ref[0, start:start + NA_BAND].reshape(tk, LANES)
                lo_l = jnp.where(second, lo[2 * ap + 1], lo[2 * ap])
                ok = (c_idx >= lo_l) & (c_idx < lo_l + NA_ROWS)
                bm_scr[p, :, ap * LANES:(ap + 1) * LANES] = jnp.where(ok, blk, NEG_INF)

    tc = kc_ref.shape[1]

    def band_start(gg):
        g = pl.program_id(2) * NA_GROUPS_PER_STEP + gg
        u0 = jnp.clip(g * NA_QROWS - NA_ROWS // 2, 0, rows - NA_BAND)
        pat = jnp.where(g == 0, 0, jnp.where(g == n_groups - 1, 2, 1))
        return pl.multiple_of(u0 * GRID_W, GRID_W), pat

    def scores(gg):
        start, pat = band_start(gg)
        q = q_ref[0, gg * tq:(gg + 1) * tq, :]
        s_scr[gg % 2, 0:tk, :] = _dot_nt(k_ref[0, pl.ds(start, tk), :], q) + bm_scr[pat]
        s_scr[gg % 2, tk:tk + tc, :] = _dot_nt(kc_ref[0], q)

    def finish(gg):
        start, _ = band_start(gg)
        s_t = [s_scr[gg % 2, 0:tk, :], s_scr[gg % 2, tk:tk + tc, :]]
        values = [v_ref[0, pl.ds(start, tk), :], vc_ref[0]]
        o_ref[0, gg * tq:(gg + 1) * tq, :] = _softmax_pv_t(s_t, values).astype(o_ref.dtype)

    scores(0)
    for gg in range(1, NA_GROUPS_PER_STEP):
        scores(gg)
        finish(gg - 1)
    finish(NA_GROUPS_PER_STEP - 1)


def _na_bias_table(rel_bias, rows):
    heads, n_dr, n_dc = rel_bias.shape
    qc = np.arange(GRID_W)[:, None]
    kc = np.arange(GRID_W)[None, :]
    cs = np.clip(qc - NA_COLS // 2, 0, GRID_W - NA_COLS)
    col_ok = (kc >= cs) & (kc < cs + NA_COLS)
    lead = GRID_W - NA_COLS
    ext = jnp.pad(rel_bias.astype(F32), ((0, 0), (0, 0), (lead, lead)))
    width = 2 * GRID_W - 1
    skew = jnp.broadcast_to(ext[:, :, None, :], (heads, n_dr, GRID_W, width))
    skew = jnp.pad(skew, ((0, 0), (0, 0), (0, 0), (0, 1))).reshape(heads, n_dr, GRID_W * (width + 1))
    skew = skew[:, :, :GRID_W * width].reshape(heads, n_dr, GRID_W, width)
    toep = skew[..., GRID_W - 1:]
    toep = jnp.where(col_ok, toep, NEG_INF).transpose(0, 1, 3, 2)
    _, _, pad_lo, pad_hi = _na_patterns(rows)
    toep = jnp.pad(toep, ((0, 0), (pad_lo, pad_hi), (0, 0), (0, 0)))
    return jnp.concatenate([toep[:, 1:], toep[:, :-1]], axis=-1)


def _na_attention(qkv, kv_c, bias_table, batch):
    t = qkv.shape[0] // batch
    tc = kv_c.shape[0] // batch
    qkv = qkv.reshape(batch, t, 3 * NA_WIDTH)
    kv_c = kv_c.reshape(batch, tc, 2 * NA_WIDTH)
    tq = NA_GROUPS_PER_STEP * NA_QROWS * GRID_W
    dh = NA_HEAD_DIM
    out = pl.pallas_call(
        _na_kernel,
        grid=(NA_HEADS, batch, t // tq),
        in_specs=[
            pl.BlockSpec((1, tq, dh), lambda h, b, g: (b, g, h)),
            pl.BlockSpec((1, t, dh), lambda h, b, g: (b, 0, NA_HEADS + h)),
            pl.BlockSpec((1, t, dh), lambda h, b, g: (b, 0, 2 * NA_HEADS + h)),
            pl.BlockSpec((1, tc, dh), lambda h, b, g: (b, 0, h)),
            pl.BlockSpec((1, tc, dh), lambda h, b, g: (b, 0, NA_HEADS + h)),
            pl.BlockSpec((1,) + bias_table.shape[1:], lambda h, b, g: (h, 0, 0, 0)),
        ],
        out_specs=pl.BlockSpec((1, tq, dh), lambda h, b, g: (b, g, h)),
        out_shape=jax.ShapeDtypeStruct((batch, t, NA_WIDTH), BF16),
        scratch_shapes=[pltpu.VMEM((3, NA_BAND * GRID_W, NA_QROWS * GRID_W), F32),
                        pltpu.VMEM((2, NA_BAND * GRID_W + tc, NA_QROWS * GRID_W), F32)],
        compiler_params=_compiler_params(("parallel", "arbitrary", "arbitrary"), 40 * 2**20),
        name="na_attention",
    )(qkv, qkv, qkv, kv_c, kv_c, bias_table)
    return out.reshape(batch * t, NA_WIDTH)


def _rope_tables(t):
    pos = jnp.arange(t)
    inv = ROPE_THETA ** (-jnp.arange(ROPE_AX_PAIRS, dtype=F32) / ROPE_AX_PAIRS)
    ar = (pos // GRID_W).astype(F32)[:, None] * inv
    ac = (pos % GRID_W).astype(F32)[:, None] * inv
    ang = jnp.concatenate([ar, ar, ac, ac], axis=-1)
    cos, sin = jnp.cos(ang), jnp.sin(ang)
    first_half = (jnp.arange(MLA_ROPE) % (2 * ROPE_AX_PAIRS)) < ROPE_AX_PAIRS
    pad = ((0, 0), (0, ROPE_PAD - MLA_ROPE))
    cos_p = jnp.pad(cos, pad, constant_values=1.0)
    sin_a = jnp.pad(jnp.where(first_half, -sin, 0.0), pad)
    sin_b = jnp.pad(jnp.where(first_half, 0.0, sin), pad)
    return cos_p, sin_a, sin_b


def _identity_rope_tables(t):
    return jnp.ones((t, ROPE_PAD), F32), jnp.zeros((t, ROPE_PAD), F32), jnp.zeros((t, ROPE_PAD), F32)


def kernel(x, c, ctx, c_ctx, mod_w, mod_b, norm_mix, norm_ffn, mla_w_in, mla_q_norm, mla_w_uq, mla_kv_norm, mla_w_ukv, lru_conv_w, lru_conv_b, lru_gate_a_w, lru_gate_a_b, lru_gate_x_w, lru_gate_x_b, lru_lambda, mix_w_out, na_w_qkv, na_rel_bias, na_w_out, ffn_w_up, ffn_conv_w, ffn_conv_b, ffn_w_down, final_norm):
    batch, t, d = x.shape
    tc = ctx.shape[1]
    n_lat, n_ctx = batch * t, batch * tc

    cond_rows = 2 * SUBLANES
    cond = jnp.zeros((cond_rows, d), F32).at[:batch].set(c).at[batch].set(c_ctx)
    mod = _modulation(cond, mod_w, mod_b)

    def mod_parts(layer):
        lat = [mod[layer, :batch, k * d:(k + 1) * d].reshape(batch, 1, d) for k in range(6)]
        cx = [mod[layer, batch:batch + 1, k * d:(k + 1) * d].reshape(1, 1, d) for k in range(6)]
        return lat, cx

    x2 = x.reshape(n_lat, d)
    c2 = ctx.reshape(n_ctx, d)

    (sh1, sc1, g1, sh2, sc2, g2), (csh1, csc1, cg1, csh2, csc2, cg2) = mod_parts(0)
    w_in = mla_w_in[0]
    o_kr = MLA_Q_LORA + MLA_KV_LORA + MLA_ROPE
    w_in_p = jnp.concatenate(
        [w_in[:, :o_kr], jnp.zeros((d, ROPE_PAD - MLA_ROPE), w_in.dtype), w_in[:, o_kr:]], axis=1).astype(BF16)
    w_uq_p = jnp.pad(mla_w_uq[0].reshape(MLA_Q_LORA, MLA_HEADS, MLA_NOPE + MLA_ROPE),
                     ((0, 0), (0, 0), (0, ROPE_PAD - MLA_ROPE))).reshape(MLA_Q_LORA, MLA_HEADS * MLA_QK).astype(BF16)
    w_ukv = mla_w_ukv[0].astype(BF16)
    w_gates = jnp.concatenate([lru_gate_a_w[0, 0], lru_gate_x_w[0, 0], lru_gate_a_w[0, 1], lru_gate_x_w[0, 1]],
                              axis=-1).astype(BF16)
    w_mix = mix_w_out[0].astype(BF16)

    pm = functools.partial(_premix0, gain=norm_mix[0], w_in=w_in_p, q_norm=mla_q_norm[0], w_uq=w_uq_p,
                           kv_norm=mla_kv_norm[0], w_ukv=w_ukv)
    cos, sin_a, sin_b = _rope_tables(t)
    q_l, k_l, v_l, rx_l, ry_l = pm(x, shift=sh1, scale=sc1, cos=cos, sin_a=sin_a, sin_b=sin_b)
    cos, sin_a, sin_b = _identity_rope_tables(tc)
    q_c, k_c, v_c, rx_c, ry_c = pm(ctx, shift=csh1, scale=csc1, cos=cos, sin_a=sin_a, sin_b=sin_b)

    att_l = _mla_attention(q_l, [k_c, k_l], [v_c, v_l], tq=512, n_unit=4).reshape(n_lat, -1)
    att_c = _mla_attention(q_c, [k_c], [v_c], tq=tc, n_unit=1).reshape(n_ctx, -1)
    rec_l, rec_c = _rglru(rx_l, rx_c, ry_l, ry_c, batch, lru_conv_w[0], lru_conv_b[0], w_gates,
                          lru_gate_a_b[0], lru_gate_x_b[0], lru_lambda[0])

    x2, h2 = _mixout(att_l, rec_l, 0, 0, w_mix, x2, g1, norm_ffn[0], sh2, sc2, t)
    c2, hc2 = _mixout(att_c, rec_c, 0, 0, w_mix, c2, cg1, norm_ffn[0], csh2, csc2, tc)

    w_up_all = ffn_w_up.astype(BF16)
    w_down_all = ffn_w_down.astype(BF16)
    ffn0 = functools.partial(_ffn, w_up=w_up_all, w_down=w_down_all, layer=0, conv_w=ffn_conv_w[0],
                             conv_b=ffn_conv_b[0], final_gain=None)
    x2 = ffn0(x2, h2, g2, seq_len=t, rows_per_mod=t)
    c2 = ffn0(c2, hc2, cg2, seq_len=tc, rows_per_mod=tc)

    (sh1, sc1, g1, sh2, sc2, g2), (csh1, csc1, _, _, _, _) = mod_parts(1)
    w_qkv = na_w_qkv[0].astype(BF16)
    qkv = _premix1(x2, norm_mix[1], sh1, sc1, w_qkv, t, NA_WIDTH)
    kv_c = _premix1(c2, norm_mix[1], csh1, csc1, w_qkv, tc, 0, col0=NA_WIDTH)
    o = _na_attention(qkv, kv_c, _na_bias_table(na_rel_bias[0], t // GRID_W), batch)
    x2, h2 = _mixout(o, o, 0, 1, na_w_out[0].astype(BF16), x2, g1, norm_ffn[1], sh2, sc2, t)
    x2 = _ffn(x2, h2, g2, w_up_all, w_down_all, 1, ffn_conv_w[1], ffn_conv_b[1], final_norm,
              seq_len=t, rows_per_mod=t)
    return x2.reshape(batch, t, d)
```

```python
import functools

import numpy as np
import jax
import jax.numpy as jnp
from jax import lax
from jax.experimental import pallas as pl
from jax.experimental.pallas import tpu as pltpu

D_MODEL = 2048
DEPTH = 2
GRID_W = 64
NORM_EPS = 1e-6
NEG_INF = -1e30

MLA_HEADS = D_MODEL // 256
MLA_Q_LORA = D_MODEL // 4
MLA_KV_LORA = D_MODEL // 8
MLA_NOPE = 128
MLA_ROPE = 64
MLA_V = 128
MLA_SCALE = (MLA_NOPE + MLA_ROPE) ** -0.5
ROPE_THETA = 10000.0
ROPE_AX_PAIRS = MLA_ROPE // 4

LRU_WIDTH = D_MODEL // 2
LRU_BLOCKS = 8
LRU_BLOCK_W = LRU_WIDTH // LRU_BLOCKS
LRU_CONV = 4
LRU_C = 8.0

NA_HEADS = 16
NA_HEAD_DIM = D_MODEL // NA_HEADS
NA_WIDTH = NA_HEADS * NA_HEAD_DIM
NA_SCALE = NA_HEAD_DIM ** -0.5
NA_ROWS = 8
NA_COLS = 16

FFN_HIDDEN = (D_MODEL * 11) // 4

LANES = 128
SUBLANES = 8
VMEM_BYTES = 64 * 1024 * 1024
ROPE_PAD = LANES
MLA_QK = MLA_NOPE + ROPE_PAD

BF16 = jnp.bfloat16
F32 = jnp.float32


def _compiler_params(semantics, vmem_bytes):
    limit = min(int(vmem_bytes), VMEM_BYTES - 4 * 1024 * 1024)
    return pltpu.CompilerParams(dimension_semantics=semantics, vmem_limit_bytes=limit)


def _resident(shape, index_map):
    return pl.BlockSpec(shape, index_map, pipeline_mode=pl.Buffered(1))


def _rms(x, g):
    return x * lax.rsqrt(jnp.mean(x * x, axis=-1, keepdims=True) + NORM_EPS) * g


def _norm_mod(x, gain, shift, scale):
    return _rms(x, gain) * (1.0 + scale) + shift


def _gelu(x):
    return 0.5 * x * (1.0 + jnp.tanh(0.7978845608028654 * (x + 0.044715 * (x * x * x))))


def _dot(a, b):
    return jnp.dot(a, b, preferred_element_type=F32)


def _dot_nt(a, b):
    return lax.dot_general(a, b, (((1,), (1,)), ((), ())), preferred_element_type=F32)


def _dot_tn(a, b):
    return lax.dot_general(a, b, (((0,), (0,)), ((), ())), preferred_element_type=F32)


def _mod_kernel(c_ref, w_ref, b_ref, o_ref):
    c = c_ref[...]
    s = (c * jax.nn.sigmoid(c)).astype(BF16)
    o_ref[0] = _dot(s, w_ref[0].astype(BF16)) + b_ref[0]


def _modulation(cond, mod_w, mod_b, tn=1024):
    rows = cond.shape[0]
    depth, d, n = mod_w.shape
    return pl.pallas_call(
        _mod_kernel,
        grid=(depth, n // tn),
        in_specs=[
            pl.BlockSpec((rows, d), lambda l, j: (0, 0)),
            pl.BlockSpec((1, d, tn), lambda l, j: (l, 0, j)),
            pl.BlockSpec((1, 1, tn), lambda l, j: (l, 0, j)),
        ],
        out_specs=pl.BlockSpec((1, rows, tn), lambda l, j: (l, 0, j)),
        out_shape=jax.ShapeDtypeStruct((depth, rows, n), F32),
        compiler_params=_compiler_params(("parallel", "parallel"), 40 * 2**20),
        name="modulation",
    )(cond, mod_w, mod_b.reshape(depth, 1, n))


def _rope(x, cos, sin_a, sin_b):
    return x * cos + pltpu.roll(x, LANES - ROPE_AX_PAIRS, 1) * sin_a + pltpu.roll(x, ROPE_AX_PAIRS, 1) * sin_b


def _premix0_kernel(x_ref, gain_ref, sh_ref, sc_ref, win_ref, qn_ref, wuq_ref, kvn_ref, wukv_ref,
                    cos_ref, sina_ref, sinb_ref, q_ref, k_ref, v_ref, rx_ref, ry_ref):
    h = _norm_mod(x_ref[...], gain_ref[...], sh_ref[0], sc_ref[0]).astype(BF16)
    p = _dot(h, win_ref[...])
    o_kv = MLA_Q_LORA
    o_kr = o_kv + MLA_KV_LORA
    o_rx = o_kr + ROPE_PAD
    o_ry = o_rx + LRU_WIDTH
    rx_ref[...] = p[:, o_rx:o_ry]
    ry_ref[...] = p[:, o_ry:]
    cos, sin_a, sin_b = cos_ref[...], sina_ref[...], sinb_ref[...]
    qn = _rms(p[:, :o_kv], qn_ref[...]).astype(BF16)
    q = _dot(qn, wuq_ref[...]) * MLA_SCALE
    cn = _rms(p[:, o_kv:o_kr], kvn_ref[...]).astype(BF16)
    kv = _dot(cn, wukv_ref[...])
    kr = _rope(p[:, o_kr:o_rx], cos, sin_a, sin_b).astype(BF16)
    for hd in range(MLA_HEADS):
        qo = hd * MLA_QK
        q_ref[0, hd, :, :MLA_NOPE] = q[:, qo:qo + MLA_NOPE].astype(BF16)
        q_ref[0, hd, :, MLA_NOPE:] = _rope(q[:, qo + MLA_NOPE:qo + MLA_QK], cos, sin_a, sin_b).astype(BF16)
        ko = hd * (MLA_NOPE + MLA_V)
        k_ref[0, hd, :, :MLA_NOPE] = kv[:, ko:ko + MLA_NOPE].astype(BF16)
        k_ref[0, hd, :, MLA_NOPE:] = kr
        v_ref[0, hd] = kv[:, ko + MLA_NOPE:ko + MLA_NOPE + MLA_V].astype(BF16)


def _premix0(x, gain, shift, scale, w_in, q_norm, w_uq, kv_norm, w_ukv, cos, sin_a, sin_b, tm=256):
    b, t, d = x.shape
    n = b * t
    tpb = t // tm
    nmod = shift.shape[0]
    mod_idx = (lambda i: (i // tpb, 0, 0)) if nmod > 1 else (lambda i: (0, 0, 0))
    const2 = lambda i: (0, 0)
    in_cols = w_in.shape[1]
    head_spec = lambda w: pl.BlockSpec((1, MLA_HEADS, tm, w), lambda i: (i // tpb, 0, i % tpb, 0))
    return pl.pallas_call(
        _premix0_kernel,
        grid=(n // tm,),
        in_specs=[
            pl.BlockSpec((tm, d), lambda i: (i, 0)),
            _resident((1, d), const2),
            pl.BlockSpec((1, 1, d), mod_idx),
            pl.BlockSpec((1, 1, d), mod_idx),
            _resident((d, in_cols), const2),
            _resident((1, MLA_Q_LORA), const2),
            _resident(w_uq.shape, const2),
            _resident((1, MLA_KV_LORA), const2),
            _resident(w_ukv.shape, const2),
            pl.BlockSpec((tm, ROPE_PAD), lambda i: (i % tpb, 0)),
            pl.BlockSpec((tm, ROPE_PAD), lambda i: (i % tpb, 0)),
            pl.BlockSpec((tm, ROPE_PAD), lambda i: (i % tpb, 0)),
        ],
        out_specs=[
            head_spec(MLA_QK), head_spec(MLA_QK), head_spec(MLA_V),
            pl.BlockSpec((tm, LRU_WIDTH), lambda i: (i, 0)),
            pl.BlockSpec((tm, LRU_WIDTH), lambda i: (i, 0)),
        ],
        out_shape=[
            jax.ShapeDtypeStruct((b, MLA_HEADS, t, MLA_QK), BF16),
            jax.ShapeDtypeStruct((b, MLA_HEADS, t, MLA_QK), BF16),
            jax.ShapeDtypeStruct((b, MLA_HEADS, t, MLA_V), BF16),
            jax.ShapeDtypeStruct((n, LRU_WIDTH), F32),
            jax.ShapeDtypeStruct((n, LRU_WIDTH), F32),
        ],
        compiler_params=_compiler_params(("parallel",), 48 * 2**20),
        name="premix0",
    )(x.reshape(n, d), gain.reshape(1, d), shift, scale, w_in, q_norm.reshape(1, -1), w_uq,
      kv_norm.reshape(1, -1), w_ukv, cos, sin_a, sin_b)


def _softmax_pv_t(scores_t, values):
    m = functools.reduce(jnp.maximum, [jnp.max(x, axis=0, keepdims=True) for x in scores_t])
    p = [jnp.exp(x - m) for x in scores_t]
    l = functools.reduce(jnp.add, [jnp.sum(x, axis=0, keepdims=True) for x in p])
    o_t = functools.reduce(jnp.add, [_dot_tn(v, x.astype(BF16)) for x, v in zip(p, values)])
    return (o_t / l).T


def _attn_kernel(*refs, n_seg, tq):
    q_ref = refs[0]
    k_refs = refs[1:1 + n_seg]
    v_refs = refs[1 + n_seg:1 + 2 * n_seg]
    o_ref = refs[1 + 2 * n_seg]
    s_scr = refs[2 + 2 * n_seg]
    n_unit = q_ref.shape[2] // tq
    seg_rows = []
    row = 0
    for k in k_refs:
        seg_rows.append(slice(row, row + k.shape[2]))
        row += k.shape[2]

    def scores(u):
        q = q_ref[0, 0, u * tq:(u + 1) * tq, :]
        for k, rows in zip(k_refs, seg_rows):
            s_scr[u % 2, rows, :] = _dot_nt(k[0, 0], q)

    def finish(u):
        s_t = [s_scr[u % 2, rows, :] for rows in seg_rows]
        o_ref[0, u * tq:(u + 1) * tq, :] = _softmax_pv_t(s_t, [v[0, 0] for v in v_refs]).astype(o_ref.dtype)

    scores(0)
    for u in range(1, n_unit):
        scores(u)
        finish(u - 1)
    finish(n_unit - 1)


def _mla_attention(q, ks, vs, tq, n_unit):
    b, h, t, dk = q.shape
    dv = vs[0].shape[-1]
    n_seg = len(ks)
    tb = tq * n_unit
    kv_spec = lambda a: pl.BlockSpec((1, 1) + a.shape[2:], lambda bi, hi, qi: (bi, hi, 0, 0))
    return pl.pallas_call(
        functools.partial(_attn_kernel, n_seg=n_seg, tq=tq),
        grid=(b, h, t // tb),
        in_specs=[pl.BlockSpec((1, 1, tb, dk), lambda bi, hi, qi: (bi, hi, qi, 0))]
        + [kv_spec(a) for a in ks] + [kv_spec(a) for a in vs],
        out_specs=pl.BlockSpec((1, tb, dv), lambda bi, hi, qi: (bi, qi, hi)),
        out_shape=jax.ShapeDtypeStruct((b, t, h * dv), BF16),
        scratch_shapes=[pltpu.VMEM((2, sum(a.shape[2] for a in ks), tq), F32)],
        compiler_params=_compiler_params(("parallel", "parallel", "parallel"), 40 * 2**20),
        name="mla_attention",
    )(q, *ks, *vs)


def _scan_tile(a, b, row, reverse):
    for d in (1, 2, 4):
        if reverse:
            keep = row < SUBLANES - d
            shift = SUBLANES - d
        else:
            keep = row >= d
            shift = d
        a_s = jnp.where(keep, pltpu.roll(a, shift, 0), 1.0)
        b_s = jnp.where(keep, pltpu.roll(b, shift, 0), 0.0)
        b = a * b_s + b
        a = a * a_s
    return a, b


def _rglru_kernel(rxl_ref, rxc_ref, ryl_ref, ryc_ref, cw_ref, cb_ref, wg_ref, gab_ref, gxb_ref, lam_ref,
                  ol_ref, oc_ref, xpad, af, bf, ab, bb, *, chunk):
    t = rxl_ref.shape[0]
    tc = rxc_ref.shape[0]
    width = rxl_ref.shape[1]
    cw = cw_ref[...]
    cb = cb_ref[...]
    z = -lam_ref[...]
    sp = jnp.maximum(z, 0.0) + jnp.log1p(jnp.exp(-jnp.abs(z)))
    gab = gab_ref[...]
    gxb = gxb_ref[...]
    pad = SUBLANES

    def coeffs(x_ref, n, off_f, off_b):
        xpad[0:pad, :] = jnp.zeros((pad, width), F32)
        xpad[pad:pad + n, :] = x_ref[...]
        xpad[pad + n:2 * pad + n, :] = jnp.zeros((pad, width), F32)
        for c0 in range(0, n, chunk):
            xc = cb
            for tap in range(LRU_CONV):
                xc = xc + cw[tap:tap + 1] * xpad[pad - 2 + tap + c0:pad - 2 + tap + c0 + chunk, :]
            xb = xc.astype(BF16)
            for blk in range(width // LRU_BLOCK_W):
                sl = slice(blk * LRU_BLOCK_W, (blk + 1) * LRU_BLOCK_W)
                g = _dot(xb[:, sl], wg_ref[blk])
                xs = xc[:, sl]
                for d, (a_ref, b_ref, off) in enumerate(((af, bf, off_f), (ab, bb, off_b))):
                    g0 = 2 * d * LRU_BLOCK_W
                    r = jax.nn.sigmoid(g[:, g0:g0 + LRU_BLOCK_W] + gab[d:d + 1, sl])
                    i = jax.nn.sigmoid(g[:, g0 + LRU_BLOCK_W:g0 + 2 * LRU_BLOCK_W] + gxb[d:d + 1, sl])
                    log_a = -LRU_C * r * sp[d:d + 1, sl]
                    a = jnp.exp(log_a)
                    one_minus_a2 = -jnp.tanh(log_a) * (a * a + 1.0)
                    a_ref[off + c0:off + c0 + chunk, sl] = a
                    root = jnp.where(one_minus_a2 > 0.0, one_minus_a2 * lax.rsqrt(one_minus_a2), 0.0)
                    b_ref[off + c0:off + c0 + chunk, sl] = root * (i * xs)

    coeffs(rxc_ref, tc, 0, t)
    coeffs(rxl_ref, t, tc, 0)

    n_tiles = (t + tc) // SUBLANES
    row = lax.broadcasted_iota(jnp.int32, (SUBLANES, width), 0)

    def body(k, carry):
        hf, hb = carry
        r0 = pl.multiple_of(k * SUBLANES, SUBLANES)
        a, b = _scan_tile(af[pl.ds(r0, SUBLANES), :], bf[pl.ds(r0, SUBLANES), :], row, False)
        h = a * hf + b
        bf[pl.ds(r0, SUBLANES), :] = h
        hf = h[SUBLANES - 1:SUBLANES]
        r1 = pl.multiple_of((n_tiles - 1 - k) * SUBLANES, SUBLANES)
        a, b = _scan_tile(ab[pl.ds(r1, SUBLANES), :], bb[pl.ds(r1, SUBLANES), :], row, True)
        h = a * hb + b
        bb[pl.ds(r1, SUBLANES), :] = h
        hb = h[0:1]
        return hf, hb

    zero = jnp.zeros((1, width), F32)
    lax.fori_loop(0, n_tiles, body, (zero, zero), unroll=4)

    for c0 in range(0, t, chunk):
        rec = bf[tc + c0:tc + c0 + chunk, :] + bb[c0:c0 + chunk, :]
        ol_ref[c0:c0 + chunk, :] = (rec * _gelu(ryl_ref[c0:c0 + chunk, :])).astype(ol_ref.dtype)
    for c0 in range(0, tc, chunk):
        rec = bf[c0:c0 + chunk, :] + bb[t + c0:t + c0 + chunk, :]
        oc_ref[c0:c0 + chunk, :] = (rec * _gelu(ryc_ref[c0:c0 + chunk, :])).astype(oc_ref.dtype)


def _rglru(rx_l, rx_c, ry_l, ry_c, batch, conv_w, conv_b, w_gates, ga_b, gx_b, lam, width=256, chunk=256):
    t = rx_l.shape[0] // batch
    tc = rx_c.shape[0] // batch
    nblk = width // LRU_BLOCK_W
    lat = pl.BlockSpec((t, width), lambda bi, ci: (bi, ci))
    ctx = pl.BlockSpec((tc, width), lambda bi, ci: (bi, ci))
    par = lambda rows: pl.BlockSpec((rows, width), lambda bi, ci: (0, ci))
    seq = t + tc
    return pl.pallas_call(
        functools.partial(_rglru_kernel, chunk=chunk),
        grid=(batch, LRU_WIDTH // width),
        in_specs=[lat, ctx, lat, ctx, par(LRU_CONV), par(1),
                  pl.BlockSpec((nblk,) + w_gates.shape[1:], lambda bi, ci: (ci, 0, 0)),
                  par(2), par(2), par(2)],
        out_specs=[lat, ctx],
        out_shape=[jax.ShapeDtypeStruct(rx_l.shape, BF16), jax.ShapeDtypeStruct(rx_c.shape, BF16)],
        scratch_shapes=[pltpu.VMEM((t + 2 * SUBLANES, width), F32)] + [pltpu.VMEM((seq, width), F32)] * 4,
        compiler_params=_compiler_params(("parallel", "parallel"), 40 * 2**20),
        name="rglru",
    )(rx_l, rx_c, ry_l, ry_c, conv_w, conv_b.reshape(1, -1), w_gates, ga_b, gx_b, lam)


def _mixout_kernel(a_ref, b_ref, wa_ref, wb_ref, x_ref, g_ref, gain_ref, sh_ref, sc_ref, o_ref, h_ref):
    tm = x_ref.shape[0]
    for r0 in range(0, tm, tm // 2):
        rs = slice(r0, r0 + tm // 2)
        y = x_ref[rs, :] + g_ref[0] * (_dot(a_ref[rs, :], wa_ref[...]) + _dot(b_ref[rs, :], wb_ref[...]))
        o_ref[rs, :] = y
        h_ref[rs, :] = _norm_mod(y, gain_ref[...], sh_ref[0], sc_ref[0]).astype(h_ref.dtype)


def _mixout(a, b, a_blk, b_blk, w, x, gate, ffn_gain, ffn_shift, ffn_scale, rows_per_mod, tm=512):
    n, d = x.shape
    kh = w.shape[0] // 2
    nmod = gate.shape[0]
    mod_idx = (lambda i: ((i * tm) // rows_per_mod, 0, 0)) if nmod > 1 else (lambda i: (0, 0, 0))
    row_spec = pl.BlockSpec((tm, d), lambda i: (i, 0))
    mod_spec = pl.BlockSpec((1, 1, d), mod_idx)
    return pl.pallas_call(
        _mixout_kernel,
        grid=(n // tm,),
        in_specs=[
            pl.BlockSpec((tm, kh), lambda i: (i, a_blk)),
            pl.BlockSpec((tm, kh), lambda i: (i, b_blk)),
            _resident((kh, d), lambda i: (0, 0)),
            _resident((kh, d), lambda i: (1, 0)),
            row_spec, mod_spec,
            _resident((1, d), lambda i: (0, 0)),
            mod_spec, mod_spec,
        ],
        out_specs=[row_spec, row_spec],
        out_shape=[jax.ShapeDtypeStruct((n, d), F32), jax.ShapeDtypeStruct((n, d), BF16)],
        compiler_params=_compiler_params(("parallel",), 48 * 2**20),
        name="mixout",
    )(a, b, w, w, x, gate, ffn_gain.reshape(1, d), ffn_shift, ffn_scale)


def _ffn_kernel(h_ref, hp_ref, hn_ref, xc_ref, gt_ref, wg_ref, wu_ref, cw_ref, cb_ref,
                wd_ref, fn_ref, o_ref, gpad, act_scr, *y_scr, seq_len, nj, nd):
    i = pl.program_id(0)
    j = pl.program_id(1)
    tm = h_ref.shape[0]
    th = act_scr.shape[2]
    td = wd_ref.shape[2]
    halo = SUBLANES

    @pl.when(j < nj)
    def _():
        h = h_ref[...]
        edge = jnp.concatenate([hp_ref[...].astype(F32)[halo:], hn_ref[...].astype(F32)[:halo]], axis=0)
        g_ext = _dot(jnp.concatenate([h, edge.astype(BF16)], axis=0), wg_ref[0])
        g = g_ext[:tm]
        gpad[halo - 1:halo, :] = g_ext[tm + halo - 1:tm + halo]
        gpad[halo:halo + tm, :] = g
        gpad[halo + tm:halo + tm + 1, :] = g_ext[tm + halo:tm + halo + 1]
        pos = (i * tm + lax.broadcasted_iota(jnp.int32, (tm, 1), 0)) & (seq_len - 1)
        g_prev = jnp.where(pos == 0, 0.0, gpad[halo - 1:halo - 1 + tm, :])
        g_next = jnp.where(pos == seq_len - 1, 0.0, gpad[halo + 1:halo + 1 + tm, :])
        cw = cw_ref[...]
        conv = cb_ref[...] + cw[0:1] * g_prev + cw[1:2] * g + cw[2:3] * g_next
        act_scr[j] = (_gelu(conv) * _dot(h, wu_ref[0])).astype(BF16)

    @pl.when(j >= nj)
    def _():
        gate = gt_ref[0]
        for r0 in range(0, tm, tm // 2):
            rs = slice(r0, r0 + tm // 2)
            acc = _dot(act_scr[0, rs, :], wd_ref[0, 0:th, :])
            for k in range(1, nj):
                acc = acc + _dot(act_scr[k, rs, :], wd_ref[0, k * th:(k + 1) * th, :])
            y = xc_ref[rs, :] + gate * acc
            if y_scr:
                y_scr[0][j - nj, rs, :] = y
            else:
                o_ref[rs, :] = y

    if y_scr:
        @pl.when(j == nj + nd - 1)
        def _():
            ys = y_scr[0]
            ssq = functools.reduce(jnp.add, [jnp.sum(ys[k] * ys[k], axis=-1, keepdims=True) for k in range(nd)])
            inv = lax.rsqrt(ssq / (nd * td) + NORM_EPS)
            for k in range(nd):
                o_ref[:, k * td:(k + 1) * td] = ys[k] * inv * fn_ref[:, k * td:(k + 1) * td]


def _ffn(x, h, gate, w_up, w_down, layer, conv_w, conv_b, final_gain, seq_len, rows_per_mod,
         tm=512, th=512, td=512):
    n, d = x.shape
    hidden = w_down.shape[1]
    nj = hidden // th
    nd = d // td
    final_norm = final_gain is not None
    assert seq_len & (seq_len - 1) == 0 and (tm % seq_len == 0 or seq_len % tm == 0)
    nmod = gate.shape[0]
    mod_row = (lambda i: (i * tm) // rows_per_mod) if nmod > 1 else (lambda i: 0)
    edge_rows = 2 * SUBLANES
    bpt = tm // edge_rows
    last_blk = n // edge_rows - 1
    up_col = lambda j: jnp.minimum(j, nj - 1)
    down_col = lambda j: jnp.maximum(j - nj, 0)
    wd_col = lambda j: jnp.where(j == 0, nd - 1, down_col(j))
    if final_norm:
        out_spec = pl.BlockSpec((tm, d), lambda i, j: (i, 0))
        y_scratch = [pltpu.VMEM((nd, tm, td), F32)]
    else:
        out_spec = pl.BlockSpec((tm, td), lambda i, j: (i, down_col(j)))
        y_scratch = []
        final_gain = jnp.zeros((d,), F32)
    return pl.pallas_call(
        functools.partial(_ffn_kernel, seq_len=seq_len, nj=nj, nd=nd),
        grid=(n // tm, nj + nd),
        in_specs=[
            pl.BlockSpec((tm, d), lambda i, j: (i, 0)),
            pl.BlockSpec((edge_rows, d), lambda i, j: (jnp.maximum(i * bpt - 1, 0), 0)),
            pl.BlockSpec((edge_rows, d), lambda i, j: (jnp.minimum((i + 1) * bpt, last_blk), 0)),
            pl.BlockSpec((tm, td), lambda i, j: (i, down_col(j))),
            pl.BlockSpec((1, 1, td), lambda i, j: (mod_row(i), 0, down_col(j))),
            pl.BlockSpec((1, d, th), lambda i, j: (layer, 0, up_col(j))),
            pl.BlockSpec((1, d, th), lambda i, j: (layer, 0, up_col(j) + nj)),
            pl.BlockSpec((conv_w.shape[0], th), lambda i, j: (0, up_col(j))),
            pl.BlockSpec((1, th), lambda i, j: (0, up_col(j))),
            pl.BlockSpec((1, hidden, td), lambda i, j: (layer, 0, wd_col(j))),
            _resident((1, d), lambda i, j: (0, 0)),
        ],
        out_specs=out_spec,
        out_shape=jax.ShapeDtypeStruct((n, d), F32),
        scratch_shapes=[
            pltpu.VMEM((tm + 2 * SUBLANES, th), F32),
            pltpu.VMEM((nj, tm, th), BF16),
        ] + y_scratch,
        compiler_params=_compiler_params(("parallel", "arbitrary"), 56 * 2**20),
        name="conv_ffn",
    )(h, h, h, x, gate, w_up, w_up, conv_w, conv_b.reshape(1, -1), w_down, final_gain.reshape(1, d))


def _premix1_kernel(x_ref, gain_ref, sh_ref, sc_ref, w_ref, o_ref, h_scr, *, n_q_tiles):
    j = pl.program_id(1)
    tm = x_ref.shape[0]
    scale = jnp.where(j < n_q_tiles, NA_SCALE, 1.0) if n_q_tiles else None

    def project(h, rows):
        y = _dot(h, w_ref[...])
        o_ref[rows, :] = (y if scale is None else y * scale).astype(o_ref.dtype)

    @pl.when(j == 0)
    def _():
        for r0 in range(0, tm, tm // 4):
            rows = slice(r0, r0 + tm // 4)
            h = _norm_mod(x_ref[rows, :], gain_ref[...], sh_ref[0], sc_ref[0]).astype(BF16)
            h_scr[rows, :] = h
            project(h, rows)

    @pl.when(j > 0)
    def _():
        project(h_scr[...], slice(None))


def _premix1(x, gain, shift, scale, w, rows_per_mod, n_q_cols, col0=0, tm=1024, tn=1024):
    n, d = x.shape
    cols = w.shape[1] - col0
    j0 = col0 // tn
    nmod = shift.shape[0]
    mod_idx = (lambda i, j: ((i * tm) // rows_per_mod, 0, 0)) if nmod > 1 else (lambda i, j: (0, 0, 0))
    return pl.pallas_call(
        functools.partial(_premix1_kernel, n_q_tiles=n_q_cols // tn),
        grid=(n // tm, cols // tn),
        in_specs=[
            pl.BlockSpec((tm, d), lambda i, j: (i, 0)),
            _resident((1, d), lambda i, j: (0, 0)),
            pl.BlockSpec((1, 1, d), mod_idx),
            pl.BlockSpec((1, 1, d), mod_idx),
            pl.BlockSpec((d, tn), lambda i, j: (0, j + j0)),
        ],
        out_specs=pl.BlockSpec((tm, tn), lambda i, j: (i, j)),
        out_shape=jax.ShapeDtypeStruct((n, cols), BF16),
        scratch_shapes=[pltpu.VMEM((tm, d), BF16)],
        compiler_params=_compiler_params(("parallel", "arbitrary"), 48 * 2**20),
        name="premix1",
    )(x, gain.reshape(1, d), shift, scale, w)


NA_QROWS = 8
NA_BAND = NA_QROWS + NA_ROWS
NA_GROUPS_PER_STEP = 4


def _na_patterns(rows):
    n_groups = rows // NA_QROWS
    assert (n_groups - 2) * NA_QROWS - NA_ROWS // 2 <= rows - NA_BAND
    a = np.arange(NA_QROWS)
    offs, los = [], []
    for g in (0, 1, n_groups - 1):
        r0 = g * NA_QROWS
        u0 = int(np.clip(r0 - NA_ROWS // 2, 0, rows - NA_BAND))
        los.append([int(v) for v in np.clip(r0 + a - NA_ROWS // 2, 0, rows - NA_ROWS) - u0])
        offs.append(u0 - r0 + NA_ROWS - 1)
    pad_lo = max(0, NA_QROWS - 1 - min(offs))
    pad_hi = max(0, max(offs) + NA_BAND - (2 * NA_ROWS - 1))
    return offs, los, pad_lo, pad_hi


def _na_kernel(q_ref, k_ref, v_ref, kc_ref, vc_ref, tbl_ref, o_ref, bm_scr, s_scr):
    rows = k_ref.shape[1] // GRID_W
    n_groups = rows // NA_QROWS
    tq = NA_QROWS * GRID_W
    tk = NA_BAND * GRID_W

    @pl.when((pl.program_id(1) == 0) & (pl.program_id(2) == 0))
    def _():
        offs, los, pad_lo, _ = _na_patterns(rows)
        c_idx = lax.broadcasted_iota(jnp.int32, (tk, LANES), 0) // GRID_W
        second = lax.broadcasted_iota(jnp.int32, (tk, LANES), 1) >= GRID_W
        for p, (off, lo) in enumerate(zip(offs, los)):
            for ap in range(NA_QROWS // 2):
                start = off - 2 * ap + pad_lo - 1
                blk = tbl_ref[0, start:start + NA_BAND].reshape(tk, LANES)
                lo_l = jnp.where(second, lo[2 * ap + 1], lo[2 * ap])
                ok = (c_idx >= lo_l) & (c_idx < lo_l + NA_ROWS)
                bm_scr[p, :, ap * LANES:(ap + 1) * LANES] = jnp.where(ok, blk, NEG_INF)

    tc = kc_ref.shape[1]

    def band_start(gg):
        g = pl.program_id(2) * NA_GROUPS_PER_STEP + gg
        u0 = jnp.clip(g * NA_QROWS - NA_ROWS // 2, 0, rows - NA_BAND)
        pat = jnp.where(g == 0, 0, jnp.where(g == n_groups - 1, 2, 1))
        return pl.multiple_of(u0 * GRID_W, GRID_W), pat

    def scores(gg):
        start, pat = band_start(gg)
        q = q_ref[0, gg * tq:(gg + 1) * tq, :]
        s_scr[gg % 2, 0:tk, :] = _dot_nt(k_ref[0, pl.ds(start, tk), :], q) + bm_scr[pat]
        s_scr[gg % 2, tk:tk + tc, :] = _dot_nt(kc_ref[0], q)

    def finish(gg):
        start, _ = band_start(gg)
        s_t = [s_scr[gg % 2, 0:tk, :], s_scr[gg % 2, tk:tk + tc, :]]
        values = [v_ref[0, pl.ds(start, tk), :], vc_ref[0]]
        o_ref[0, gg * tq:(gg + 1) * tq, :] = _softmax_pv_t(s_t, values).astype(o_ref.dtype)

    scores(0)
    for gg in range(1, NA_GROUPS_PER_STEP):
        scores(gg)
        finish(gg - 1)
    finish(NA_GROUPS_PER_STEP - 1)


def _na_bias_table(rel_bias, rows):
    heads, n_dr, n_dc = rel_bias.shape
    qc = np.arange(GRID_W)[:, None]
    kc = np.arange(GRID_W)[None, :]
    cs = np.clip(qc - NA_COLS // 2, 0, GRID_W - NA_COLS)
    col_ok = (kc >= cs) & (kc < cs + NA_COLS)
    lead = GRID_W - NA_COLS
    ext = jnp.pad(rel_bias.astype(F32), ((0, 0), (0, 0), (lead, lead)))
    width = 2 * GRID_W - 1
    skew = jnp.broadcast_to(ext[:, :, None, :], (heads, n_dr, GRID_W, width))
    skew = jnp.pad(skew, ((0, 0), (0, 0), (0, 0), (0, 1))).reshape(heads, n_dr, GRID_W * (width + 1))
    skew = skew[:, :, :GRID_W * width].reshape(heads, n_dr, GRID_W, width)
    toep = skew[..., GRID_W - 1:]
    toep = jnp.where(col_ok, toep, NEG_INF).transpose(0, 1, 3, 2)
    _, _, pad_lo, pad_hi = _na_patterns(rows)
    toep = jnp.pad(toep, ((0, 0), (pad_lo, pad_hi), (0, 0), (0, 0)))
    return jnp.concatenate([toep[:, 1:], toep[:, :-1]], axis=-1)


def _na_attention(qkv, kv_c, bias_table, batch):
    t = qkv.shape[0] // batch
    tc = kv_c.shape[0] // batch
    qkv = qkv.reshape(batch, t, 3 * NA_WIDTH)
    kv_c = kv_c.reshape(batch, tc, 2 * NA_WIDTH)
    tq = NA_GROUPS_PER_STEP * NA_QROWS * GRID_W
    dh = NA_HEAD_DIM
    out = pl.pallas_call(
        _na_kernel,
        grid=(NA_HEADS, batch, t // tq),
        in_specs=[
            pl.BlockSpec((1, tq, dh), lambda h, b, g: (b, g, h)),
            pl.BlockSpec((1, t, dh), lambda h, b, g: (b, 0, NA_HEADS + h)),
            pl.BlockSpec((1, t, dh), lambda h, b, g: (b, 0, 2 * NA_HEADS + h)),
            pl.BlockSpec((1, tc, dh), lambda h, b, g: (b, 0, h)),
            pl.BlockSpec((1, tc, dh), lambda h, b, g: (b, 0, NA_HEADS + h)),
            pl.BlockSpec((1,) + bias_table.shape[1:], lambda h, b, g: (h, 0, 0, 0)),
        ],
        out_specs=pl.BlockSpec((1, tq, dh), lambda h, b, g: (b, g, h)),
        out_shape=jax.ShapeDtypeStruct((batch, t, NA_WIDTH), BF16),
        scratch_shapes=[pltpu.VMEM((3, NA_BAND * GRID_W, NA_QROWS * GRID_W), F32),
                        pltpu.VMEM((2, NA_BAND * GRID_W + tc, NA_QROWS * GRID_W), F32)],
        compiler_params=_compiler_params(("parallel", "arbitrary", "arbitrary"), 40 * 2**20),
        name="na_attention",
    )(qkv, qkv, qkv, kv_c, kv_c, bias_table)
    return out.reshape(batch * t, NA_WIDTH)


def _rope_tables(t):
    pos = jnp.arange(t)
    inv = ROPE_THETA ** (-jnp.arange(ROPE_AX_PAIRS, dtype=F32) / ROPE_AX_PAIRS)
    ar = (pos // GRID_W).astype(F32)[:, None] * inv
    ac = (pos % GRID_W).astype(F32)[:, None] * inv
    ang = jnp.concatenate([ar, ar, ac, ac], axis=-1)
    cos, sin = jnp.cos(ang), jnp.sin(ang)
    first_half = (jnp.arange(MLA_ROPE) % (2 * ROPE_AX_PAIRS)) < ROPE_AX_PAIRS
    pad = ((0, 0), (0, ROPE_PAD - MLA_ROPE))
    cos_p = jnp.pad(cos, pad, constant_values=1.0)
    sin_a = jnp.pad(jnp.where(first_half, -sin, 0.0), pad)
    sin_b = jnp.pad(jnp.where(first_half, 0.0, sin), pad)
    return cos_p, sin_a, sin_b


def _identity_rope_tables(t):
    return jnp.ones((t, ROPE_PAD), F32), jnp.zeros((t, ROPE_PAD), F32), jnp.zeros((t, ROPE_PAD), F32)


def kernel(x, c, ctx, c_ctx, mod_w, mod_b, norm_mix, norm_ffn, mla_w_in, mla_q_norm, mla_w_uq, mla_kv_norm, mla_w_ukv, lru_conv_w, lru_conv_b, lru_gate_a_w, lru_gate_a_b, lru_gate_x_w, lru_gate_x_b, lru_lambda, mix_w_out, na_w_qkv, na_rel_bias, na_w_out, ffn_w_up, ffn_conv_w, ffn_conv_b, ffn_w_down, final_norm):
    batch, t, d = x.shape
    tc = ctx.shape[1]
    n_lat, n_ctx = batch * t, batch * tc

    cond_rows = 2 * SUBLANES
    cond = jnp.zeros((cond_rows, d), F32).at[:batch].set(c).at[batch].set(c_ctx)
    mod = _modulation(cond, mod_w, mod_b)

    def mod_parts(layer):
        lat = [mod[layer, :batch, k * d:(k + 1) * d].reshape(batch, 1, d) for k in range(6)]
        cx = [mod[layer, batch:batch + 1, k * d:(k + 1) * d].reshape(1, 1, d) for k in range(6)]
        return lat, cx

    x2 = x.reshape(n_lat, d)
    c2 = ctx.reshape(n_ctx, d)

    (sh1, sc1, g1, sh2, sc2, g2), (csh1, csc1, cg1, csh2, csc2, cg2) = mod_parts(0)
    w_in = mla_w_in[0]
    o_kr = MLA_Q_LORA + MLA_KV_LORA + MLA_ROPE
    w_in_p = jnp.concatenate(
        [w_in[:, :o_kr], jnp.zeros((d, ROPE_PAD - MLA_ROPE), w_in.dtype), w_in[:, o_kr:]], axis=1).astype(BF16)
    w_uq_p = jnp.pad(mla_w_uq[0].reshape(MLA_Q_LORA, MLA_HEADS, MLA_NOPE + MLA_ROPE),
                     ((0, 0), (0, 0), (0, ROPE_PAD - MLA_ROPE))).reshape(MLA_Q_LORA, MLA_HEADS * MLA_QK).astype(BF16)
    w_ukv = mla_w_ukv[0].astype(BF16)
    w_gates = jnp.concatenate([lru_gate_a_w[0, 0], lru_gate_x_w[0, 0], lru_gate_a_w[0, 1], lru_gate_x_w[0, 1]],
                              axis=-1).astype(BF16)
    w_mix = mix_w_out[0].astype(BF16)

    pm = functools.partial(_premix0, gain=norm_mix[0], w_in=w_in_p, q_norm=mla_q_norm[0], w_uq=w_uq_p,
                           kv_norm=mla_kv_norm[0], w_ukv=w_ukv)
    cos, sin_a, sin_b = _rope_tables(t)
    q_l, k_l, v_l, rx_l, ry_l = pm(x, shift=sh1, scale=sc1, cos=cos, sin_a=sin_a, sin_b=sin_b)
    cos, sin_a, sin_b = _identity_rope_tables(tc)
    q_c, k_c, v_c, rx_c, ry_c = pm(ctx, shift=csh1, scale=csc1, cos=cos, sin_a=sin_a, sin_b=sin_b)

    att_l = _mla_attention(q_l, [k_c, k_l], [v_c, v_l], tq=512, n_unit=4).reshape(n_lat, -1)
    att_c = _mla_attention(q_c, [k_c], [v_c], tq=tc, n_unit=1).reshape(n_ctx, -1)
    rec_l, rec_c = _rglru(rx_l, rx_c, ry_l, ry_c, batch, lru_conv_w[0], lru_conv_b[0], w_gates,
                          lru_gate_a_b[0], lru_gate_x_b[0], lru_lambda[0])

    x2, h2 = _mixout(att_l, rec_l, 0, 0, w_mix, x2, g1, norm_ffn[0], sh2, sc2, t)
    c2, hc2 = _mixout(att_c, rec_c, 0, 0, w_mix, c2, cg1, norm_ffn[0], csh2, csc2, tc)

    w_up_all = ffn_w_up.astype(BF16)
    w_down_all = ffn_w_down.astype(BF16)
    ffn0 = functools.partial(_ffn, w_up=w_up_all, w_down=w_down_all, layer=0, conv_w=ffn_conv_w[0],
                             conv_b=ffn_conv_b[0], final_gain=None)
    x2 = ffn0(x2, h2, g2, seq_len=t, rows_per_mod=t)
    c2 = ffn0(c2, hc2, cg2, seq_len=tc, rows_per_mod=tc)

    (sh1, sc1, g1, sh2, sc2, g2), (csh1, csc1, _, _, _, _) = mod_parts(1)
    w_qkv = na_w_qkv[0].astype(BF16)
    qkv = _premix1(x2, norm_mix[1], sh1, sc1, w_qkv, t, NA_WIDTH)
    kv_c = _premix1(c2, norm_mix[1], csh1, csc1, w_qkv, tc, 0, col0=NA_WIDTH)
    o = _na_attention(qkv, kv_c, _na_bias_table(na_rel_bias[0], t // GRID_W), batch)
    x2, h2 = _mixout(o, o, 0, 1, na_w_out[0].astype(BF16), x2, g1, norm_ffn[1], sh2, sc2, t)
    x2 = _ffn(x2, h2, g2, w_up_all, w_down_all, 1, ffn_conv_w[1], ffn_conv_b[1], final_norm,
              seq_len=t, rows_per_mod=t)
    return x2.reshape(batch, t, d)
```
